```python
import numpy as np
import jax
import jax.numpy as jnp
from jax import lax

D_MODEL = 1024
BATCH = 16
SEQ = 4096
DEPTH = 2

CONV_WIDTH = 4
NORM_EPS = 1e-6
SSD_HEAD_DIM = 64
SSD_HEADS = D_MODEL // SSD_HEAD_DIM
SSD_WIDTH = SSD_HEADS * SSD_HEAD_DIM
SSD_GROUPS = 2
SSD_STATE = 128
SSD_CHUNK = 64
SSD_CONV_DIM = SSD_WIDTH + 2 * SSD_GROUPS * SSD_STATE
HGRN_HEAD_DIM = 128
HGRN_HEADS = D_MODEL // HGRN_HEAD_DIM
HGRN_WIDTH = HGRN_HEADS * HGRN_HEAD_DIM
HGRN_CHUNK = 64
GATE_LOGIT_CLIP = 30.0
LRU_WIDTH = D_MODEL
LRU_BLOCKS = 4
LRU_BLOCK_DIM = LRU_WIDTH // LRU_BLOCKS
LRU_C = 8.0
N_BRANCH = 3
IN_SPLITS = (SSD_WIDTH, SSD_CONV_DIM, SSD_HEADS,
             HGRN_WIDTH, HGRN_WIDTH, HGRN_WIDTH, HGRN_WIDTH,
             LRU_WIDTH, LRU_WIDTH,
             N_BRANCH * D_MODEL)
IN_TOTAL = sum(IN_SPLITS)
N_EXPERTS = 32
TOP_K = 4
D_EXPERT = D_MODEL
SWIGLU_LIMIT = 7.0
SWIGLU_ALPHA = 1.702
MOE_BLOCK = 256

kernel_name = "hybrid_ssd_hgrn2_rglru_moe_adaln"


def rms_norm(x, g):
    xf = x.astype(jnp.float32)
    y = xf * lax.rsqrt(jnp.mean(xf * xf, axis=-1, keepdims=True) + NORM_EPS)
    return (y * g.astype(jnp.float32)).astype(x.dtype)


def causal_dwconv(x, w, b):
    k = w.shape[0]
    y = lax.conv_general_dilated(x, w[:, None, :], window_strides=(1,), padding=[(k - 1, 0)],
                                 dimension_numbers=("NWC", "WIO", "NWC"),
                                 feature_group_count=x.shape[-1])
    return y + b


def masked_decay(diff, mask):
    return jnp.where(mask, jnp.exp(jnp.where(mask, diff, 0.0)), 0.0)


def ssd_mixer(z, xbc, dt_raw, conv_w, conv_b, dt_bias, a_log, d_skip, norm_w):
    bsz, s, _ = z.shape
    nc, L, G, HG = s // SSD_CHUNK, SSD_CHUNK, SSD_GROUPS, SSD_HEADS // SSD_GROUPS
    xbc = jax.nn.silu(causal_dwconv(xbc, conv_w, conv_b).astype(jnp.float32))
    xs, bm, cm = jnp.split(xbc, [SSD_WIDTH, SSD_WIDTH + G * SSD_STATE], axis=-1)
    dt = jax.nn.softplus(dt_raw.astype(jnp.float32) + dt_bias.astype(jnp.float32))
    a = -jnp.exp(a_log.astype(jnp.float32))
    x_h = xs.reshape(bsz, nc, L, G, HG, SSD_HEAD_DIM)
    xdt = x_h * dt.reshape(bsz, nc, L, G, HG)[..., None]
    b_c = bm.reshape(bsz, nc, L, G, SSD_STATE)
    c_c = cm.reshape(bsz, nc, L, G, SSD_STATE)
    acs = jnp.moveaxis(jnp.cumsum((dt * a).reshape(bsz, nc, L, G, HG), axis=2), 2, -1)
    causal = jnp.tril(jnp.ones((L, L), dtype=bool))
    decay_ls = masked_decay(acs[..., :, None] - acs[..., None, :], causal)
    cb = jnp.einsum("bclgn,bcsgn->bcgls", c_c, b_c)
    y_diag = jnp.einsum("bcgls,bcghls,bcsghp->bclghp", cb, decay_ls, xdt)
    decay_to_end = jnp.exp(acs[..., -1:] - acs)
    chunk_states = jnp.einsum("bclgn,bcghl,bclghp->bcghpn", b_c, decay_to_end, xdt)
    chunk_decay = jnp.exp(acs[..., -1])

    def step(h, inp):
        st, dec = inp
        return h * dec[..., None, None] + st, h

    h0 = jnp.zeros((bsz, G, HG, SSD_HEAD_DIM, SSD_STATE), jnp.float32)
    _, prev = lax.scan(step, h0, (jnp.moveaxis(chunk_states, 1, 0), jnp.moveaxis(chunk_decay, 1, 0)))
    prev = jnp.moveaxis(prev, 0, 1)
    y_off = jnp.einsum("bclgn,bcghl,bcghpn->bclghp", c_c, jnp.exp(acs), prev)
    y = y_diag + y_off + x_h * d_skip.astype(jnp.float32).reshape(G, HG, 1)
    y = y.reshape(bsz, s, SSD_WIDTH) * jax.nn.silu(z.astype(jnp.float32))
    y = y.reshape(bsz, s, G, SSD_WIDTH // G)
    y = y * lax.rsqrt(jnp.mean(y * y, axis=-1, keepdims=True) + NORM_EPS)
    y = y.reshape(bsz, s, SSD_WIDTH) * norm_w.astype(jnp.float32)
    return y.astype(z.dtype)


def hgrn2_mixer(q, f, i, g, lower_bound, norm_w):
    bsz, s, _ = q.shape
    nc, L, H, dk = s // HGRN_CHUNK, HGRN_CHUNK, HGRN_HEADS, HGRN_HEAD_DIM
    qf = jax.nn.silu(q.astype(jnp.float32))
    ff = f.astype(jnp.float32)
    lb = lower_bound.astype(jnp.float32)
    log_f = jax.nn.log_sigmoid(ff) + jnp.log1p(lb * jnp.exp(-jnp.maximum(ff, -GATE_LOGIT_CLIP)))
    log_f = jnp.minimum(log_f, 0.0)
    kf = -jnp.expm1(log_f)

    def chunks(t):
        return jnp.moveaxis(t.astype(jnp.float32).reshape(bsz, nc, L, H, dk), 1, 0)

    causal = jnp.tril(jnp.ones((L, L), dtype=bool))[None, :, :, None, None]

    def chunk_step(state, inp):
        qc, kc, vc, lfc = inp
        bcum = jnp.cumsum(lfc, axis=1)
        pair = masked_decay(bcum[:, :, None] - bcum[:, None, :], causal)
        attn = jnp.einsum("bthk,bshk,btshk->bhts", qc, kc, pair)
        o = jnp.einsum("bhts,bshv->bthv", attn, vc) + jnp.einsum("bthk,bhkv->bthv", qc * jnp.exp(bcum), state)
        b_end = bcum[:, -1]
        new_state = state * jnp.exp(b_end)[..., None] + jnp.einsum(
            "bshk,bshv->bhkv", kc * jnp.exp(b_end[:, None] - bcum), vc)
        return new_state, o

    s0 = jnp.zeros((bsz, H, dk, dk), jnp.float32)
    _, o = lax.scan(chunk_step, s0, (chunks(qf), chunks(kf), chunks(i), chunks(log_f)))
    o = jnp.moveaxis(o, 0, 1).reshape(bsz, s, H, dk)
    o = o * lax.rsqrt(jnp.mean(o * o, axis=-1, keepdims=True) + NORM_EPS) * norm_w.astype(jnp.float32)
    o = o.reshape(bsz, s, HGRN_WIDTH) * jax.nn.silu(g.astype(jnp.float32))
    return o.astype(q.dtype)


def rglru_mixer(gate_in, x_in, conv_w, conv_b, w_a, b_a, w_x, b_x, lam):
    bsz, s, _ = x_in.shape
    xc = causal_dwconv(x_in, conv_w, conv_b).astype(jnp.float32)
    xb = xc.reshape(bsz, s, LRU_BLOCKS, LRU_BLOCK_DIM)
    r = jax.nn.sigmoid(jnp.einsum("bshi,hij->bshj", xb, w_a.astype(jnp.float32)) + b_a.astype(jnp.float32))
    ig = jax.nn.sigmoid(jnp.einsum("bshi,hij->bshj", xb, w_x.astype(jnp.float32)) + b_x.astype(jnp.float32))
    r = r.reshape(bsz, s, LRU_WIDTH)
    ig = ig.reshape(bsz, s, LRU_WIDTH)
    log_a = -LRU_C * r * jax.nn.softplus(-lam.astype(jnp.float32))
    a = jnp.exp(log_a)
    mult = jnp.sqrt(jnp.maximum(-jnp.expm1(2.0 * log_a), 0.0))
    mult = jnp.where((jnp.arange(s) == 0)[None, :, None], 1.0, mult)
    u = mult * ig * xc

    def combine(left, right):
        a1, b1 = left
        a2, b2 = right
        return a1 * a2, a2 * b1 + b2

    _, h = lax.associative_scan(combine, (a, u), axis=1)
    return (h * jax.nn.gelu(gate_in.astype(jnp.float32))).astype(x_in.dtype)


def hybrid_mixer(u, w_in, ssd_conv_w, ssd_conv_b, ssd_dt_bias, ssd_a_log, ssd_d, ssd_norm,
                 lower_bound, hgrn_norm, lru_conv_w, lru_conv_b, lru_wa, lru_ba, lru_wx, lru_bx, lru_lambda,
                 w_br_ssd, w_br_hgrn, w_br_lru, w_out):
    proj = u @ w_in
    offsets = np.cumsum(IN_SPLITS)[:-1].tolist()
    z, xbc, dt, hq, hf, hi, hg, lgate, lx, gates = jnp.split(proj, offsets, axis=-1)
    y_ssd = ssd_mixer(z, xbc, dt, ssd_conv_w, ssd_conv_b, ssd_dt_bias, ssd_a_log, ssd_d, ssd_norm)
    y_hgrn = hgrn2_mixer(hq, hf, hi, hg, lower_bound, hgrn_norm)
    y_lru = rglru_mixer(lgate, lx, lru_conv_w, lru_conv_b, lru_wa, lru_ba, lru_wx, lru_bx, lru_lambda)
    g_ssd, g_hgrn, g_lru = jnp.split(jax.nn.sigmoid(gates.astype(jnp.float32)).astype(u.dtype), N_BRANCH, axis=-1)
    merged = g_ssd * (y_ssd @ w_br_ssd) + g_hgrn * (y_hgrn @ w_br_hgrn) + g_lru * (y_lru @ w_br_lru)
    return merged @ w_out


def moe_ffn(h, router_w, router_b, w_gu, b_gu, w_down, b_down):
    bsz, s, d = h.shape
    t = h.reshape(-1, d)
    n_tok = t.shape[0]
    logits = (t @ router_w + router_b).astype(jnp.float32)
    top_logit, top_idx = lax.top_k(logits, TOP_K)
    top_w = jax.nn.softmax(top_logit, axis=-1)
    n_assign = n_tok * TOP_K
    flat_e = top_idx.reshape(-1).astype(jnp.int32)
    flat_tok = jnp.repeat(jnp.arange(n_tok, dtype=jnp.int32), TOP_K)
    flat_w = top_w.reshape(-1)
    order = jnp.argsort(flat_e)
    se = flat_e[order]
    counts = jnp.bincount(flat_e, length=N_EXPERTS)
    padded = (counts + MOE_BLOCK - 1) // MOE_BLOCK * MOE_BLOCK
    start = jnp.cumsum(counts) - counts
    pstart = jnp.cumsum(padded) - padded
    dest = pstart[se] + jnp.arange(n_assign, dtype=jnp.int32) - start[se]
    n_blocks = -(-n_assign // MOE_BLOCK) + N_EXPERTS
    n_rows = n_blocks * MOE_BLOCK
    row_tok = jnp.zeros((n_rows,), jnp.int32).at[dest].set(flat_tok[order])
    row_w = jnp.zeros((n_rows,), jnp.float32).at[dest].set(flat_w[order])
    block_start = jnp.arange(n_blocks, dtype=jnp.int32) * MOE_BLOCK
    block_e = jnp.minimum(jnp.searchsorted(jnp.cumsum(padded), block_start, side="right"), N_EXPERTS - 1)

    def expert_block(args):
        tok, e = args
        xb = t[tok]
        gu = (xb @ w_gu[e] + b_gu[e]).astype(jnp.float32)
        glu = jnp.minimum(gu[:, :D_EXPERT], SWIGLU_LIMIT)
        lin = jnp.clip(gu[:, D_EXPERT:], -SWIGLU_LIMIT, SWIGLU_LIMIT)
        act = (glu * jax.nn.sigmoid(SWIGLU_ALPHA * glu) * (lin + 1.0)).astype(xb.dtype)
        return act @ w_down[e] + b_down[e]

    ys = lax.map(expert_block, (row_tok.reshape(n_blocks, MOE_BLOCK), block_e)).reshape(n_rows, d)
    out = jnp.zeros((n_tok, d), jnp.float32).at[row_tok].add(ys.astype(jnp.float32) * row_w[:, None])
    return out.astype(h.dtype).reshape(bsz, s, d)


def setup_inputs(seed: int = 0) -> dict:
    key = jax.random.key(seed)
    ks = jax.random.split(key, 40)

    def nrm(k, shape, scale):
        return jax.random.normal(k, shape, jnp.float32) * scale

    dt0 = jnp.exp(jax.random.uniform(ks[9], (DEPTH, SSD_HEADS), jnp.float32, np.log(1e-3), np.log(1e-1)))
    a0 = jax.random.uniform(ks[22], (DEPTH, LRU_WIDTH), jnp.float32, 0.9, 0.999) ** (1.0 / LRU_C)
    return {
        "x": nrm(ks[0], (BATCH, SEQ, D_MODEL), 1.0),
        "c": nrm(ks[1], (BATCH, D_MODEL), 1.0),
        "ada_w": nrm(ks[2], (DEPTH, D_MODEL, 6 * D_MODEL), 0.5 * D_MODEL ** -0.5),
        "ada_b": nrm(ks[3], (DEPTH, 6 * D_MODEL), 0.02),
        "norm_mix": 1.0 + nrm(ks[4], (DEPTH, D_MODEL), 0.02),
        "norm_ffn": 1.0 + nrm(ks[5], (DEPTH, D_MODEL), 0.02),
        "w_in": nrm(ks[6], (DEPTH, D_MODEL, IN_TOTAL), D_MODEL ** -0.5),
        "ssd_conv_w": nrm(ks[7], (DEPTH, CONV_WIDTH, SSD_CONV_DIM), CONV_WIDTH ** -0.5),
        "ssd_conv_b": nrm(ks[8], (DEPTH, SSD_CONV_DIM), 0.02),
        "ssd_dt_bias": dt0 + jnp.log(-jnp.expm1(-dt0)),
        "ssd_a_log": jnp.log(jax.random.uniform(ks[10], (DEPTH, SSD_HEADS), jnp.float32, 1.0, 16.0)),
        "ssd_d": 1.0 + nrm(ks[11], (DEPTH, SSD_HEADS), 0.02),
        "ssd_norm": 1.0 + nrm(ks[12], (DEPTH, SSD_WIDTH), 0.02),
        "hgrn_lb": nrm(ks[13], (DEPTH, HGRN_WIDTH), 0.5),
        "hgrn_norm": 1.0 + nrm(ks[14], (DEPTH, HGRN_HEADS, HGRN_HEAD_DIM), 0.02),
        "lru_conv_w": nrm(ks[15], (DEPTH, CONV_WIDTH, LRU_WIDTH), CONV_WIDTH ** -0.5),
        "lru_conv_b": nrm(ks[16], (DEPTH, LRU_WIDTH), 0.02),
        "lru_wa": nrm(ks[17], (DEPTH, LRU_BLOCKS, LRU_BLOCK_DIM, LRU_BLOCK_DIM), LRU_BLOCK_DIM ** -0.5),
        "lru_ba": nrm(ks[18], (DEPTH, LRU_BLOCKS, LRU_BLOCK_DIM), 0.02),
        "lru_wx": nrm(ks[19], (DEPTH, LRU_BLOCKS, LRU_BLOCK_DIM, LRU_BLOCK_DIM), LRU_BLOCK_DIM ** -0.5),
        "lru_bx": nrm(ks[20], (DEPTH, LRU_BLOCKS, LRU_BLOCK_DIM), 0.02),
        "lru_lambda": jnp.log(a0) - jnp.log1p(-a0),
        "w_br_ssd": nrm(ks[23], (DEPTH, SSD_WIDTH, D_MODEL), SSD_WIDTH ** -0.5),
        "w_br_hgrn": nrm(ks[24], (DEPTH, HGRN_WIDTH, D_MODEL), HGRN_WIDTH ** -0.5),
        "w_br_lru": nrm(ks[25], (DEPTH, LRU_WIDTH, D_MODEL), LRU_WIDTH ** -0.5),
        "w_out": nrm(ks[26], (DEPTH, D_MODEL, D_MODEL), D_MODEL ** -0.5),
        "router_w": nrm(ks[27], (DEPTH, D_MODEL, N_EXPERTS), D_MODEL ** -0.5),
        "router_b": nrm(ks[28], (DEPTH, N_EXPERTS), 0.01),
        "moe_w_gu": nrm(ks[29], (DEPTH, N_EXPERTS, D_MODEL, 2 * D_EXPERT), D_MODEL ** -0.5),
        "moe_b_gu": nrm(ks[30], (DEPTH, N_EXPERTS, 2 * D_EXPERT), 0.02),
        "moe_w_down": nrm(ks[31], (DEPTH, N_EXPERTS, D_EXPERT, D_MODEL), D_EXPERT ** -0.5),
        "moe_b_down": nrm(ks[32], (DEPTH, N_EXPERTS, D_MODEL), 0.02),
        "final_norm": 1.0 + nrm(ks[33], (D_MODEL,), 0.02),
    }


def reference(x, c, ada_w, ada_b, norm_mix, norm_ffn, w_in, ssd_conv_w, ssd_conv_b, ssd_dt_bias,
              ssd_a_log, ssd_d, ssd_norm, hgrn_lb, hgrn_norm, lru_conv_w, lru_conv_b, lru_wa, lru_ba,
              lru_wx, lru_bx, lru_lambda, w_br_ssd, w_br_hgrn, w_br_lru, w_out, router_w, router_b,
              moe_w_gu, moe_b_gu, moe_w_down, moe_b_down, final_norm):
    lb_soft = jax.nn.softmax(hgrn_lb.astype(jnp.float32), axis=0)
    lower_bounds = jnp.cumsum(lb_soft, axis=0) - lb_soft[0]
    cond = jax.nn.silu(c)
    for l in range(DEPTH):
        mod = cond @ ada_w[l] + ada_b[l]
        sh1, sc1, g1, sh2, sc2, g2 = jnp.split(mod[:, None, :], 6, axis=-1)
        u = rms_norm(x, norm_mix[l]) * (1.0 + sc1) + sh1
        x = x + g1 * hybrid_mixer(u, w_in[l], ssd_conv_w[l], ssd_conv_b[l], ssd_dt_bias[l], ssd_a_log[l],
                                  ssd_d[l], ssd_norm[l], lower_bounds[l], hgrn_norm[l], lru_conv_w[l],
                                  lru_conv_b[l], lru_wa[l], lru_ba[l], lru_wx[l], lru_bx[l], lru_lambda[l],
                                  w_br_ssd[l], w_br_hgrn[l], w_br_lru[l], w_out[l])
        v = rms_norm(x, norm_ffn[l]) * (1.0 + sc2) + sh2
        x = x + g2 * moe_ffn(v, router_w[l], router_b[l], moe_w_gu[l], moe_b_gu[l], moe_w_down[l], moe_b_down[l])
    return rms_norm(x, final_norm)
```

```python
import functools

import numpy as np
import jax
import jax.numpy as jnp
from jax import lax
from jax.experimental import pallas as pl
from jax.experimental.pallas import tpu as pltpu

F32 = jnp.float32
BF16 = jnp.bfloat16

NORM_EPS = 1e-6
CONV_WIDTH = 4
CHUNK = 64
SSD_HEADS = 16
SSD_HEAD_DIM = 64
SSD_GROUPS = 2
SSD_STATE = 128
HGRN_HEADS = 8
HGRN_HEAD_DIM = 128
GATE_LOGIT_CLIP = 30.0
LRU_BLOCKS = 4
LRU_C = 8.0
N_EXPERTS = 32
TOP_K = 4
SWIGLU_LIMIT = 7.0
SWIGLU_ALPHA = 1.702
HGRN_LEVELS = (64, 32, 16, 8, 4, 2)

LANE = 128
SUBLANE = 8
VMEM_LIMIT = 48 * 1024 * 1024

SEG = 1024
COL_Z, COL_XS, COL_BCDT, COL_HQ, COL_HF, COL_HI, COL_HG, COL_LG, COL_LX, COL_G0 = range(10)
N_PROJ = 12 * SEG


def _sigmoid(x):
    return jax.nn.sigmoid(x)


def _silu(x):
    return x * _sigmoid(x)


def _softplus(x):
    return jnp.maximum(x, 0.0) + jnp.log1p(jnp.exp(-jnp.abs(x)))


def _log_sigmoid(x):
    return jnp.minimum(x, 0.0) - jnp.log1p(jnp.exp(-jnp.abs(x)))


def _neg_expm1(x):
    t = jnp.tanh(0.5 * x)
    return -2.0 * t / (1.0 - t)


def _gelu_tanh(x):
    return 0.5 * x * (1.0 + jnp.tanh(0.7978845608028654 * (x + 0.044715 * (x * x * x))))


def _split3(x):
    hi = x.astype(BF16)
    r1 = x - hi.astype(F32)
    mid = r1.astype(BF16)
    lo = (r1 - mid.astype(F32)).astype(BF16)
    return hi, mid, lo


def _exact_left_dot(p_bf16, x):
    hi, mid, lo = _split3(x)
    d = lambda a: jnp.dot(p_bf16, a, preferred_element_type=F32)
    return d(hi) + d(mid) + d(lo)


def _dot_nt(a, b):
    return lax.dot_general(a, b, (((1,), (1,)), ((), ())), preferred_element_type=F32)


def _dot_tn(a, b):
    return lax.dot_general(a, b, (((0,), (0,)), ((), ())), preferred_element_type=F32)


def _causal_conv(raw, carry_ref, w_ref, b_ref):
    n = raw.shape[0]
    ext = jnp.concatenate([carry_ref[...], raw], axis=0)
    y = b_ref[...] + w_ref[CONV_WIDTH - 1:CONV_WIDTH, :] * raw
    for k in range(CONV_WIDTH - 1):
        shifted = pltpu.roll(ext, CONV_WIDTH - 1 - k, 0)[SUBLANE:SUBLANE + n]
        y = y + w_ref[k:k + 1, :] * shifted
    carry_ref[...] = raw[n - SUBLANE:n]
    return y


def _adaln_kernel(c_ref, w_ref, b_ref, o_ref):
    c = c_ref[...]
    o_ref[0] = jnp.dot(_silu(c), w_ref[0], precision=lax.Precision.HIGHEST,
                       preferred_element_type=F32) + b_ref[0]


def _adaln(c, ada_w, ada_b):
    depth, d, n = ada_w.shape
    bsz = c.shape[0]
    tn = 1024
    return pl.pallas_call(
        _adaln_kernel,
        out_shape=jax.ShapeDtypeStruct((depth, bsz, n), F32),
        grid=(depth, n // tn),
        in_specs=[pl.BlockSpec((bsz, d), lambda l, j: (0, 0)),
                  pl.BlockSpec((1, d, tn), lambda l, j: (l, 0, j)),
                  pl.BlockSpec((1, 1, tn), lambda l, j: (l, 0, j))],
        out_specs=pl.BlockSpec((1, bsz, tn), lambda l, j: (l, 0, j)),
        compiler_params=pltpu.CompilerParams(dimension_semantics=("arbitrary", "arbitrary")),
        name="adaln",
    )(c, ada_w, ada_b.reshape(depth, 1, n))


def _inproj_kernel(x_ref, sc_ref, sh_ref, g_ref, w_ref, o_ref, u_ref):
    @pl.when(pl.program_id(1) == 0)
    def _():
        x = x_ref[...]
        y = x * lax.rsqrt(jnp.mean(x * x, axis=-1, keepdims=True) + NORM_EPS) * g_ref[...]
        u_ref[...] = (y * (1.0 + sc_ref[0]) + sh_ref[0]).astype(BF16)

    o_ref[...] = jnp.dot(u_ref[...], w_ref[...], preferred_element_type=F32)


def _inproj(x2, sc, sh, g, w, seq):
    t, d = x2.shape
    n = w.shape[1]
    tm = min(1024, seq)
    tn = 1024
    per_b = seq // tm
    return pl.pallas_call(
        _inproj_kernel,
        out_shape=jax.ShapeDtypeStruct((t, n), F32),
        grid=(t // tm, n // tn),
        in_specs=[pl.BlockSpec((tm, d), lambda i, j: (i, 0)),
                  pl.BlockSpec((1, 1, d), lambda i, j: (i // per_b, 0, 0)),
                  pl.BlockSpec((1, 1, d), lambda i, j: (i // per_b, 0, 0)),
                  pl.BlockSpec((1, d), lambda i, j: (0, 0)),
                  pl.BlockSpec((d, tn), lambda i, j: (0, j))],
        out_specs=pl.BlockSpec((tm, tn), lambda i, j: (i, j)),
        scratch_shapes=[pltpu.VMEM((tm, d), BF16)],
        compiler_params=pltpu.CompilerParams(dimension_semantics=("arbitrary", "arbitrary"),
                                             vmem_limit_bytes=VMEM_LIMIT),
        name="inproj",
    )(x2, sc, sh, g, w)


def _ssd_kernel(z_ref, xs_ref, bcdt_ref, cwx_ref, cbx_ref, cwb_ref, cbb_ref, dtb_ref, alog_ref,
                dskip_ref, nw_ref, tri_ref, o_ref, cx_ref, cb_ref, st_ref, xs_s, bc_s, dt_s, y_s):
    ts = z_ref.shape[0]
    p, nst, hpg = SSD_HEAD_DIM, SSD_STATE, SSD_HEADS // SSD_GROUPS

    @pl.when(pl.program_id(1) == 0)
    def _():
        cx_ref[...] = jnp.zeros_like(cx_ref)
        cb_ref[...] = jnp.zeros_like(cb_ref)
        st_ref[...] = jnp.zeros_like(st_ref)

    xs_s[...] = _silu(_causal_conv(xs_ref[...], cx_ref, cwx_ref, cbx_ref))
    bc_s[...] = _silu(_causal_conv(bcdt_ref[:, 0:4 * nst], cb_ref, cwb_ref, cbb_ref))
    dt_s[...] = _softplus(bcdt_ref[:, 4 * nst:4 * nst + LANE] + dtb_ref[...])
    a_neg = -jnp.exp(alog_ref[...])
    li = lax.broadcasted_iota(jnp.int32, (CHUNK, CHUNK), 0)
    si = lax.broadcasted_iota(jnp.int32, (CHUNK, CHUNK), 1)
    causal = li >= si

    def chunk_body(c, carry):
        r0 = pl.multiple_of(c * CHUNK, CHUNK)
        rows = pl.ds(r0, CHUNK)
        dt = dt_s[rows, :]
        acs = _exact_left_dot(tri_ref[...], dt * a_neg)
        acs_t = acs.T
        for g in range(SSD_GROUPS):
            b_g = bc_s[rows, g * nst:(g + 1) * nst]
            c_g = bc_s[rows, (SSD_GROUPS + g) * nst:(SSD_GROUPS + g + 1) * nst]
            cbm = _dot_nt(c_g.astype(BF16), b_g.astype(BF16))
            for hh in range(hpg):
                h = g * hpg + hh
                col = acs[:, h:h + 1]
                row = acs_t[h:h + 1, :]
                last = acs[CHUNK - 1:CHUNK, h:h + 1]
                decay = jnp.where(causal, jnp.exp(jnp.where(causal, col - row, 0.0)), 0.0)
                x_h = xs_s[rows, h * p:(h + 1) * p]
                xdt = (x_h * dt[:, h:h + 1]).astype(BF16)
                y = jnp.dot((cbm * decay).astype(BF16), xdt, preferred_element_type=F32)
                state = st_ref[h]
                y = y + _dot_nt((c_g * jnp.exp(col)).astype(BF16), state.astype(BF16))
                bd = (b_g * jnp.exp(last - col)).astype(BF16)
                st_ref[h] = state * jnp.exp(last) + _dot_tn(xdt, bd)
                y_s[rows, h * p:(h + 1) * p] = y + x_h * dskip_ref[:, h * p:(h + 1) * p]
        return carry

    lax.fori_loop(0, ts // CHUNK, chunk_body, 0)

    y = y_s[...] * _silu(z_ref[...])
    half = y.shape[1] // SSD_GROUPS
    outs = []
    for g in range(SSD_GROUPS):
        yg = y[:, g * half:(g + 1) * half]
        outs.append(yg * lax.rsqrt(jnp.mean(yg * yg, axis=-1, keepdims=True) + NORM_EPS))
    o_ref[...] = (jnp.concatenate(outs, axis=1) * nw_ref[...]).astype(o_ref.dtype)


def _ssd(proj, bsz, seq, p):
    t = proj.shape[0]
    ts = min(256, seq)
    per_b = seq // ts
    d = SEG
    row = lambda col: pl.BlockSpec((ts, d), lambda b, s: (b * per_b + s, col))
    const = lambda shape: pl.BlockSpec(shape, lambda b, s: (0,) * len(shape))
    return pl.pallas_call(
        _ssd_kernel,
        out_shape=jax.ShapeDtypeStruct((t, d), BF16),
        grid=(bsz, per_b),
        in_specs=[row(COL_Z), row(COL_XS), row(COL_BCDT),
                  const((CONV_WIDTH, d)), const((1, d)), const((CONV_WIDTH, 4 * SSD_STATE)),
                  const((1, 4 * SSD_STATE)), const((1, LANE)), const((1, LANE)), const((1, d)),
                  const((1, d)), const((CHUNK, CHUNK))],
        out_specs=pl.BlockSpec((ts, d), lambda b, s: (b * per_b + s, 0)),
        scratch_shapes=[pltpu.VMEM((SUBLANE, d), F32), pltpu.VMEM((SUBLANE, 4 * SSD_STATE), F32),
                        pltpu.VMEM((SSD_HEADS, SSD_HEAD_DIM, SSD_STATE), F32),
                        pltpu.VMEM((ts, d), F32), pltpu.VMEM((ts, 4 * SSD_STATE), F32),
                        pltpu.VMEM((ts, LANE), F32), pltpu.VMEM((ts, d), F32)],
        compiler_params=pltpu.CompilerParams(dimension_semantics=("arbitrary", "arbitrary"),
                                             vmem_limit_bytes=VMEM_LIMIT),
        name="ssd",
    )(proj, proj, proj, p["cwx"], p["cbx"], p["cwb"], p["cbb"], p["dtb"], p["alog"], p["dskip"],
      p["ssd_norm"], p["tri"])


def _hgrn_tables():
    n = CHUNK
    r = np.arange(n)[:, None]
    j = np.arange(n)[None, :]
    mats = [j <= r, j > r]
    for w in HGRN_LEVELS:
        mid = r - r % w + w // 2
        mats.append(np.where(r >= mid, (j >= mid) & (j <= r), (j > r) & (j < mid)))
    return np.concatenate(mats, axis=0).astype(np.float32)


def _hgrn_kernel(q_ref, f_ref, i_ref, g_ref, lb_ref, nw_ref, pm_ref, o_ref, st_ref, o_s):
    ts = q_ref.shape[0]
    dk = HGRN_HEAD_DIM

    @pl.when(pl.program_id(1) == 0)
    def _():
        st_ref[...] = jnp.zeros_like(st_ref)

    ti = lax.broadcasted_iota(jnp.int32, (CHUNK, CHUNK), 0)
    si = lax.broadcasted_iota(jnp.int32, (CHUNK, CHUNK), 1)
    txs = ti ^ si
    ri = lax.broadcasted_iota(jnp.int32, (CHUNK, 1), 0)

    def chunk_body(c, carry):
        r0 = pl.multiple_of(c * CHUNK, CHUNK)
        rows = pl.ds(r0, CHUNK)
        ff = f_ref[rows, :]
        log_f = _log_sigmoid(ff) + jnp.log1p(lb_ref[...] * jnp.exp(-jnp.maximum(ff, -GATE_LOGIT_CLIP)))
        log_f = jnp.minimum(log_f, 0.0)
        kf = _neg_expm1(log_f)
        qf = _silu(q_ref[rows, :])
        vv = i_ref[rows, :]
        gsum = _exact_left_dot(pm_ref[...], log_f)
        for h in range(HGRN_HEADS):
            hs = slice(h * dk, (h + 1) * dk)
            qh, kh = qf[:, hs], kf[:, hs]
            vh = vv[:, hs].astype(BF16)
            attn = jnp.where(txs == 0, _dot_nt(qh.astype(BF16), kh.astype(BF16)), 0.0)
            for lv, w in enumerate(HGRN_LEVELS):
                e = jnp.exp(gsum[(2 + lv) * CHUNK:(3 + lv) * CHUNK, hs])
                upper = (ri & (w // 2)) != 0
                qt = jnp.where(upper, qh * e, 0.0).astype(BF16)
                kt = jnp.where(upper, 0.0, kh * e).astype(BF16)
                attn = attn + jnp.where(txs < w, _dot_nt(qt, kt), 0.0)
            o = jnp.dot(attn.astype(BF16), vh, preferred_element_type=F32)
            state = st_ref[h]
            qs = (qh * jnp.exp(gsum[0:CHUNK, hs])).astype(BF16)
            o = o + _dot_nt(qs, state.astype(BF16))
            kd = (kh * jnp.exp(gsum[CHUNK:2 * CHUNK, hs])).astype(BF16)
            b_end = gsum[CHUNK - 1:CHUNK, hs]
            st_ref[h] = state * jnp.exp(b_end) + _dot_tn(vh, kd)
            o = o * lax.rsqrt(jnp.mean(o * o, axis=-1, keepdims=True) + NORM_EPS)
            o_s[rows, hs] = o
        return carry

    lax.fori_loop(0, ts // CHUNK, chunk_body, 0)
    o_ref[...] = (o_s[...] * nw_ref[...] * _silu(g_ref[...])).astype(o_ref.dtype)


def _hgrn(proj, bsz, seq, p):
    t = proj.shape[0]
    ts = min(256, seq)
    per_b = seq // ts
    d = SEG
    row = lambda col: pl.BlockSpec((ts, d), lambda b, s: (b * per_b + s, col))
    const = lambda shape: pl.BlockSpec(shape, lambda b, s: (0,) * len(shape))
    n_tab = (2 + len(HGRN_LEVELS)) * CHUNK
    return pl.pallas_call(
        _hgrn_kernel,
        out_shape=jax.ShapeDtypeStruct((t, d), BF16),
        grid=(bsz, per_b),
        in_specs=[row(COL_HQ), row(COL_HF), row(COL_HI), row(COL_HG),
                  const((1, d)), const((1, d)), const((n_tab, CHUNK))],
        out_specs=pl.BlockSpec((ts, d), lambda b, s: (b * per_b + s, 0)),
        scratch_shapes=[pltpu.VMEM((HGRN_HEADS, HGRN_HEAD_DIM, HGRN_HEAD_DIM), F32),
                        pltpu.VMEM((ts, d), F32)],
        compiler_params=pltpu.CompilerParams(dimension_semantics=("arbitrary", "arbitrary"),
                                             vmem_limit_bytes=VMEM_LIMIT),
        name="hgrn2",
    )(proj, proj, proj, proj, p["lb"], p["hgrn_norm"], p["hgrn_tab"])


def _lru_kernel(gate_ref, x_ref, cw_ref, cb_ref, wa_ref, ba_ref, wx_ref, bx_ref, lam_ref,
                o_ref, cx_ref, h_ref):
    ts, d = x_ref.shape
    bd = d // LRU_BLOCKS
    first = pl.program_id(1) == 0

    @pl.when(first)
    def _():
        cx_ref[...] = jnp.zeros_like(cx_ref)
        h_ref[...] = jnp.zeros_like(h_ref)

    xc = _causal_conv(x_ref[...], cx_ref, cw_ref, cb_ref)
    xcb = xc.astype(BF16)
    ra, rx = [], []
    for i in range(LRU_BLOCKS):
        blk = xcb[:, i * bd:(i + 1) * bd]
        ra.append(jnp.dot(blk, wa_ref[i], preferred_element_type=F32))
        rx.append(jnp.dot(blk, wx_ref[i], preferred_element_type=F32))
    r = _sigmoid(jnp.concatenate(ra, axis=1) + ba_ref[...])
    ig = _sigmoid(jnp.concatenate(rx, axis=1) + bx_ref[...])
    log_a = -LRU_C * r * _softplus(-lam_ref[...])
    a = jnp.exp(log_a)
    mult = jnp.sqrt(jnp.maximum(_neg_expm1(2.0 * log_a), 0.0))
    rowi = lax.broadcasted_iota(jnp.int32, (ts, 1), 0)
    mult = jnp.where(jnp.logical_and(first, rowi == 0), 1.0, mult)
    u = mult * ig * xc
    sh = 1
    while sh < ts:
        keep = rowi >= sh
        a_prev = jnp.where(keep, pltpu.roll(a, sh, 0), 1.0)
        u_prev = jnp.where(keep, pltpu.roll(u, sh, 0), 0.0)
        u = a * u_prev + u
        a = a * a_prev
        sh *= 2
    h = u + a * h_ref[0:1, :]
    h_ref[...] = jnp.broadcast_to(h[ts - 1:ts, :], h_ref.shape)
    o_ref[...] = (h * _gelu_tanh(gate_ref[...])).astype(o_ref.dtype)


def _lru(proj, bsz, seq, p):
    t = proj.shape[0]
    ts = min(256, seq)
    per_b = seq // ts
    d = SEG
    bd = d // LRU_BLOCKS
    row = lambda col: pl.BlockSpec((ts, d), lambda b, s: (b * per_b + s, col))
    const = lambda shape: pl.BlockSpec(shape, lambda b, s: (0,) * len(shape))
    return pl.pallas_call(
        _lru_kernel,
        out_shape=jax.ShapeDtypeStruct((t, d), BF16),
        grid=(bsz, per_b),
        in_specs=[row(COL_LG), row(COL_LX), const((CONV_WIDTH, d)), const((1, d)),
                  const((LRU_BLOCKS, bd, bd)), const((1, d)), const((LRU_BLOCKS, bd, bd)), const((1, d)),
                  const((1, d))],
        out_specs=pl.BlockSpec((ts, d), lambda b, s: (b * per_b + s, 0)),
        scratch_shapes=[pltpu.VMEM((SUBLANE, d), F32), pltpu.VMEM((SUBLANE, d), F32)],
        compiler_params=pltpu.CompilerParams(dimension_semantics=("arbitrary", "arbitrary"),
                                             vmem_limit_bytes=VMEM_LIMIT),
        name="rglru",
    )(proj, proj, p["lru_cw"], p["lru_cb"], p["lru_wa"], p["lru_ba"], p["lru_wx"], p["lru_bx"],
      p["lru_lam"])


def _merge_kernel(ys_ref, yh_ref, yl_ref, g0_ref, g1_ref, g2_ref, x_ref, gate_ref, sc_ref, sh_ref,
                  nf_ref, wbs_ref, wbh_ref, wbl_ref, wo_ref, rwt_ref, rb_ref, up_ref,
                  xo_ref, v_ref, idx_ref, tw_ref, rank_ref, cnt_ref, carry_ref):
    tm = x_ref.shape[0]

    @pl.when(pl.program_id(0) == 0)
    def _():
        carry_ref[...] = jnp.zeros_like(carry_ref)

    def branch(y_ref, g_ref, w_ref):
        return _sigmoid(g_ref[...]) * jnp.dot(y_ref[...], w_ref[...], preferred_element_type=F32)

    merged = branch(ys_ref, g0_ref, wbs_ref) + branch(yh_ref, g1_ref, wbh_ref) + branch(yl_ref, g2_ref, wbl_ref)
    mix = jnp.dot(merged.astype(BF16), wo_ref[...], preferred_element_type=F32)
    x = x_ref[...] + gate_ref[0] * mix
    xo_ref[...] = x
    v = x * lax.rsqrt(jnp.mean(x * x, axis=-1, keepdims=True) + NORM_EPS) * nf_ref[...]
    v = v * (1.0 + sc_ref[0]) + sh_ref[0]
    v_ref[...] = v

    v_hi = v.astype(BF16)
    v_lo = (v - v_hi.astype(F32)).astype(BF16)
    logits = (_dot_nt(rwt_ref[0], v_hi) + _dot_nt(rwt_ref[0], v_lo) + _dot_nt(rwt_ref[1], v_hi)
              + rb_ref[...])
    ei = lax.broadcasted_iota(jnp.int32, (N_EXPERTS, tm), 0)
    work = logits
    tops, sel = [], []
    for k in range(TOP_K):
        m = jnp.max(work, axis=0, keepdims=True)
        idx = jnp.min(jnp.where(work == m, ei, N_EXPERTS), axis=0, keepdims=True)
        hit = ei == idx
        tops.append(m)
        sel.append(hit)
        idx_ref[k:k + 1, :] = idx
        work = jnp.where(hit, -jnp.inf, work)
    exps = [jnp.exp(m - tops[0]) for m in tops]
    denom = exps[0] + exps[1] + exps[2] + exps[3]
    for k in range(TOP_K):
        tw_ref[k:k + 1, :] = exps[k] / denom

    member = jnp.where(sel[0] | sel[1] | sel[2] | sel[3], 1.0, 0.0)
    before = jnp.dot(member.astype(BF16), up_ref[...], preferred_element_type=F32) + carry_ref[:, 0:1]
    for k in range(TOP_K):
        rank_ref[k:k + 1, :] = jnp.sum(jnp.where(sel[k], before, 0.0), axis=0, keepdims=True).astype(jnp.int32)
    carry_ref[...] = carry_ref[...] + jnp.sum(member, axis=1, keepdims=True)
    cnt_ref[...] = carry_ref[...]


def _merge(y_ssd, y_hgrn, y_lru, proj, x2, g1, sc2, sh2, p, seq):
    t, d = x2.shape
    tm = min(512, seq)
    per_b = seq // tm
    row = lambda col: pl.BlockSpec((tm, d), lambda i: (i, col))
    mod = pl.BlockSpec((1, 1, d), lambda i: (i // per_b, 0, 0))
    const = lambda shape: pl.BlockSpec(shape, lambda i: (0,) * len(shape))
    tok = lambda: pl.BlockSpec((TOP_K, tm), lambda i: (0, i))
    return pl.pallas_call(
        _merge_kernel,
        out_shape=(jax.ShapeDtypeStruct((t, d), F32), jax.ShapeDtypeStruct((t, d), F32),
                   jax.ShapeDtypeStruct((TOP_K, t), jnp.int32), jax.ShapeDtypeStruct((TOP_K, t), F32),
                   jax.ShapeDtypeStruct((TOP_K, t), jnp.int32), jax.ShapeDtypeStruct((N_EXPERTS, LANE), F32)),
        grid=(t // tm,),
        in_specs=[row(0), row(0), row(0), row(COL_G0), row(COL_G0 + 1), row(COL_G0 + 2), row(0),
                  mod, mod, mod, const((1, d)), const((d, d)), const((d, d)), const((d, d)), const((d, d)),
                  const((2, N_EXPERTS, d)), const((N_EXPERTS, 1)), const((tm, tm))],
        out_specs=(row(0), row(0), tok(), tok(), tok(), const((N_EXPERTS, LANE))),
        scratch_shapes=[pltpu.VMEM((N_EXPERTS, LANE), F32)],
        compiler_params=pltpu.CompilerParams(dimension_semantics=("arbitrary",),
                                             vmem_limit_bytes=VMEM_LIMIT),
        name="merge_router",
    )(y_ssd, y_hgrn, y_lru, proj, proj, proj, x2, g1, sc2, sh2, p["norm_ffn"], p["w_br_ssd"],
      p["w_br_hgrn"], p["w_br_lru"], p["w_out"], p["router_wt"], p["router_b"], p["upper"])


def _dispatch_kernel(dest_hbm, v_ref, xs_in, xs_hbm, idx_s, sem_i, sem_o):
    del xs_in
    tg = v_ref.shape[0]
    i = pl.program_id(0)
    cp = pltpu.make_async_copy(dest_hbm.at[i], idx_s, sem_i)
    cp.start()
    cp.wait()

    def issue(tk, carry):
        for k in range(TOP_K):
            pos = k * tg + tk
            dst = idx_s[pos // LANE, pos % LANE]
            pltpu.make_async_copy(v_ref.at[pl.ds(tk, 1)], xs_hbm.at[pl.ds(dst, 1)], sem_o).start()
        return carry

    lax.fori_loop(0, tg, issue, 0)

    def drain(tk, carry):
        for k in range(TOP_K):
            pltpu.make_async_copy(v_ref.at[pl.ds(0, 1)], xs_hbm.at[pl.ds(0, 1)], sem_o).wait()
        return carry

    lax.fori_loop(0, tg, drain, 0)


def _dispatch(dest_tiles, v, xs_zero, tg):
    t, d = v.shape
    n_rows = xs_zero.shape[0]
    return pl.pallas_call(
        _dispatch_kernel,
        out_shape=jax.ShapeDtypeStruct((n_rows, d), F32),
        grid=(t // tg,),
        in_specs=[pl.BlockSpec(memory_space=pl.ANY),
                  pl.BlockSpec((tg, d), lambda i: (i, 0)),
                  pl.BlockSpec(memory_space=pl.ANY)],
        out_specs=pl.BlockSpec(memory_space=pl.ANY),
        scratch_shapes=[pltpu.SMEM((TOP_K * tg // LANE, LANE), jnp.int32),
                        pltpu.SemaphoreType.DMA, pltpu.SemaphoreType.DMA],
        input_output_aliases={2: 0},
        compiler_params=pltpu.CompilerParams(dimension_semantics=("arbitrary",)),
        name="moe_dispatch",
    )(dest_tiles, v, xs_zero)


def _expert_kernel(be_ref, nu_ref, x_ref, wgu_ref, bgu_ref, wd_ref, bd_ref, o_ref):
    del be_ref
    i = pl.program_id(0)
    de = wd_ref.shape[1]

    @pl.when(i < nu_ref[0])
    def _():
        gu = jnp.dot(x_ref[...].astype(BF16), wgu_ref[0], preferred_element_type=F32) + bgu_ref[0]
        glu = jnp.minimum(gu[:, :de], SWIGLU_LIMIT)
        lin = jnp.clip(gu[:, de:], -SWIGLU_LIMIT, SWIGLU_LIMIT)
        act = glu * _sigmoid(SWIGLU_ALPHA * glu) * (lin + 1.0)
        o_ref[...] = jnp.dot(act.astype(BF16), wd_ref[0], preferred_element_type=F32) + bd_ref[0]

    @pl.when(i >= nu_ref[0])
    def _():
        o_ref[...] = jnp.zeros_like(o_ref)


def _experts(block_e, n_used, xs, p, blk):
    n_rows, d = xs.shape
    n_blocks = n_rows // blk
    de = p["w_down"].shape[1]
    return pl.pallas_call(
        _expert_kernel,
        out_shape=jax.ShapeDtypeStruct((n_rows, d), F32),
        grid_spec=pltpu.PrefetchScalarGridSpec(
            num_scalar_prefetch=2,
            grid=(n_blocks,),
            in_specs=[pl.BlockSpec((blk, d), lambda i, be, nu: (i, 0)),
                      pl.BlockSpec((1, d, 2 * de), lambda i, be, nu: (be[i], 0, 0)),
                      pl.BlockSpec((1, 1, 2 * de), lambda i, be, nu: (be[i], 0, 0)),
                      pl.BlockSpec((1, de, d), lambda i, be, nu: (be[i], 0, 0)),
                      pl.BlockSpec((1, 1, d), lambda i, be, nu: (be[i], 0, 0))],
            out_specs=pl.BlockSpec((blk, d), lambda i, be, nu: (i, 0))),
        compiler_params=pltpu.CompilerParams(dimension_semantics=("arbitrary",),
                                             vmem_limit_bytes=VMEM_LIMIT),
        name="moe_experts",
    )(block_e, n_used, xs, p["w_gu"], p["b_gu"], p["w_down"], p["b_down"])


def _combine_kernel(dest_hbm, ys_hbm, x_ref, tw_ref, gate_ref, fn_ref, o_ref, idx_s, buf, sem_i, sem_g,
                    *, final_norm):
    tc = x_ref.shape[0]
    i = pl.program_id(0)
    cp = pltpu.make_async_copy(dest_hbm.at[i], idx_s, sem_i)
    cp.start()
    cp.wait()

    def issue(tk, carry):
        for k in range(TOP_K):
            pos = k * tc + tk
            src = idx_s[pos // LANE, pos % LANE]
            pltpu.make_async_copy(ys_hbm.at[pl.ds(src, 1)], buf.at[k, pl.ds(tk, 1)], sem_g).start()
        return carry

    lax.fori_loop(0, tc, issue, 0)

    def drain(tk, carry):
        for k in range(TOP_K):
            pltpu.make_async_copy(ys_hbm.at[pl.ds(0, 1)], buf.at[0, pl.ds(0, 1)], sem_g).wait()
        return carry

    lax.fori_loop(0, tc, drain, 0)

    moe = tw_ref[:, 0:1] * buf[0]
    for k in range(1, TOP_K):
        moe = moe + tw_ref[:, k:k + 1] * buf[k]
    x = x_ref[...] + gate_ref[0] * moe
    if final_norm:
        x = x * lax.rsqrt(jnp.mean(x * x, axis=-1, keepdims=True) + NORM_EPS) * fn_ref[...]
    o_ref[...] = x


def _combine(dest_tiles, ys, x2, tw_t, g2, final_w, seq, tc, final_norm):
    t, d = x2.shape
    per_b = seq // tc
    return pl.pallas_call(
        functools.partial(_combine_kernel, final_norm=final_norm),
        out_shape=jax.ShapeDtypeStruct((t, d), F32),
        grid=(t // tc,),
        in_specs=[pl.BlockSpec(memory_space=pl.ANY), pl.BlockSpec(memory_space=pl.ANY),
                  pl.BlockSpec((tc, d), lambda i: (i, 0)),
                  pl.BlockSpec((tc, TOP_K), lambda i: (i, 0)),
                  pl.BlockSpec((1, 1, d), lambda i: (i // per_b, 0, 0)),
                  pl.BlockSpec((1, d), lambda i: (0, 0))],
        out_specs=pl.BlockSpec((tc, d), lambda i: (i, 0)),
        scratch_shapes=[pltpu.SMEM((TOP_K * tc // LANE, LANE), jnp.int32),
                        pltpu.VMEM((TOP_K, tc, d), F32),
                        pltpu.SemaphoreType.DMA, pltpu.SemaphoreType.DMA],
        compiler_params=pltpu.CompilerParams(dimension_semantics=("arbitrary",),
                                             vmem_limit_bytes=VMEM_LIMIT),
        name="moe_combine",
    )(dest_tiles, ys, x2, tw_t, g2, final_w)


def _moe(v, top_idx, top_w, rank, counts, x2, g2, final_w, p, seq, final_norm):
    t, d = v.shape
    blk = 512 if t * TOP_K >= 512 * N_EXPERTS else 128
    n_blocks = t * TOP_K // blk + N_EXPERTS
    cnt = counts[:, 0].astype(jnp.int32)
    padded = (cnt + blk - 1) // blk * blk
    pend = jnp.cumsum(padded)
    pstart = pend - padded
    dest = rank + jnp.sum(jnp.where(top_idx[None] == jnp.arange(N_EXPERTS, dtype=jnp.int32)[:, None, None],
                                    pstart[:, None, None], 0), axis=0)
    block_start = jnp.arange(n_blocks, dtype=jnp.int32) * blk
    block_e = jnp.minimum(jnp.sum(block_start[:, None] >= pend[None, :], axis=1), N_EXPERTS - 1).astype(jnp.int32)
    n_used = (pend[-1:] // blk).astype(jnp.int32)
    tg = min(256, seq)
    dest_tiles = dest.reshape(TOP_K, t // tg, tg).transpose(1, 0, 2).reshape(t // tg, TOP_K * tg // LANE, LANE)
    xs = _dispatch(dest_tiles, v, jnp.zeros((n_blocks * blk, d), F32), tg)
    ys = _experts(block_e, n_used, xs, p, blk)
    return _combine(dest_tiles, ys, x2, top_w.T, g2, final_w, seq, tg, final_norm)


def _pack_w_in(w_in):
    d = w_in.shape[0]
    sizes = (1024, 1536, 16, 1024, 1024, 1024, 1024, 1024, 1024, 3072)
    offs = np.concatenate([[0], np.cumsum(sizes)])
    z, xbc, dt, hq, hf, hi, hg, lg, lx, gates = [w_in[:, offs[k]:offs[k + 1]] for k in range(len(sizes))]
    pad = jnp.zeros((d, SEG - 512 - 16), w_in.dtype)
    return jnp.concatenate([z, xbc[:, :1024], xbc[:, 1024:], dt, pad, hq, hf, hi, hg, lg, lx, gates],
                           axis=1).astype(BF16)


def _layer_params(l, a, lower_bound):
    d = a["w_in"].shape[1]
    row = lambda v: v.reshape(1, -1).astype(F32)
    lane_pad = lambda v: jnp.pad(v.astype(F32), (0, LANE - v.shape[0])).reshape(1, LANE)
    r = np.arange(CHUNK)
    rt = min(512, a["seq"])
    return {
        "w_in": _pack_w_in(a["w_in"][l]),
        "norm_mix": row(a["norm_mix"][l]), "norm_ffn": row(a["norm_ffn"][l]),
        "cwx": a["ssd_conv_w"][l][:, :d], "cbx": row(a["ssd_conv_b"][l][:d]),
        "cwb": a["ssd_conv_w"][l][:, d:], "cbb": row(a["ssd_conv_b"][l][d:]),
        "dtb": lane_pad(a["ssd_dt_bias"][l]), "alog": lane_pad(a["ssd_a_log"][l]),
        "dskip": row(jnp.repeat(a["ssd_d"][l], SSD_HEAD_DIM)), "ssd_norm": row(a["ssd_norm"][l]),
        "tri": jnp.asarray((r[None, :] <= r[:, None]).astype(np.float32), BF16),
        "lb": row(lower_bound), "hgrn_norm": row(a["hgrn_norm"][l]),
        "hgrn_tab": jnp.asarray(_hgrn_tables(), BF16),
        "lru_cw": a["lru_conv_w"][l], "lru_cb": row(a["lru_conv_b"][l]),
        "lru_wa": a["lru_wa"][l].astype(BF16), "lru_ba": row(a["lru_ba"][l]),
        "lru_wx": a["lru_wx"][l].astype(BF16), "lru_bx": row(a["lru_bx"][l]),
        "lru_lam": row(a["lru_lambda"][l]),
        "w_br_ssd": a["w_br_ssd"][l].astype(BF16), "w_br_hgrn": a["w_br_hgrn"][l].astype(BF16),
        "w_br_lru": a["w_br_lru"][l].astype(BF16), "w_out": a["w_out"][l].astype(BF16),
        "router_wt": _router_split(a["router_w"][l]), "router_b": a["router_b"][l].reshape(-1, 1).astype(F32),
        "upper": jnp.asarray((np.arange(rt)[:, None] < np.arange(rt)[None, :]).astype(np.float32), BF16),
        "w_gu": a["moe_w_gu"][l].astype(BF16), "b_gu": a["moe_b_gu"][l][:, None, :],
        "w_down": a["moe_w_down"][l].astype(BF16), "b_down": a["moe_b_down"][l][:, None, :],
    }


def _router_split(w):
    wt = w.T.astype(F32)
    hi = wt.astype(BF16)
    lo = (wt - hi.astype(F32)).astype(BF16)
    return jnp.stack([hi, lo])


def kernel(x, c, ada_w, ada_b, norm_mix, norm_ffn, w_in, ssd_conv_w, ssd_conv_b, ssd_dt_bias, ssd_a_log, ssd_d, ssd_norm, hgrn_lb, hgrn_norm, lru_conv_w, lru_conv_b, lru_wa, lru_ba, lru_wx, lru_bx, lru_lambda, w_br_ssd, w_br_hgrn, w_br_lru, w_out, router_w, router_b, moe_w_gu, moe_b_gu, moe_w_down, moe_b_down, final_norm):
    bsz, seq, d = x.shape
    depth = ada_w.shape[0]
    a = dict(seq=seq, norm_mix=norm_mix, norm_ffn=norm_ffn, w_in=w_in, ssd_conv_w=ssd_conv_w,
             ssd_conv_b=ssd_conv_b, ssd_dt_bias=ssd_dt_bias, ssd_a_log=ssd_a_log, ssd_d=ssd_d,
             ssd_norm=ssd_norm, hgrn_norm=hgrn_norm, lru_conv_w=lru_conv_w, lru_conv_b=lru_conv_b,
             lru_wa=lru_wa, lru_ba=lru_ba, lru_wx=lru_wx, lru_bx=lru_bx, lru_lambda=lru_lambda,
             w_br_ssd=w_br_ssd, w_br_hgrn=w_br_hgrn, w_br_lru=w_br_lru, w_out=w_out, router_w=router_w,
             router_b=router_b, moe_w_gu=moe_w_gu, moe_b_gu=moe_b_gu, moe_w_down=moe_w_down,
             moe_b_down=moe_b_down)
    lb_soft = jax.nn.softmax(hgrn_lb.astype(F32), axis=0)
    lower_bounds = jnp.cumsum(lb_soft, axis=0) - lb_soft[0]
    mod = _adaln(c, ada_w, ada_b).reshape(depth, bsz, 6, 1, d)
    x2 = x.reshape(bsz * seq, d)
    fw = final_norm.reshape(1, d)
    for l in range(depth):
        p = _layer_params(l, a, lower_bounds[l])
        sh1, sc1, g1, sh2, sc2, g2 = [mod[l, :, k] for k in range(6)]
        proj = _inproj(x2, sc1, sh1, p["norm_mix"], p["w_in"], seq)
        y_ssd = _ssd(proj, bsz, seq, p)
        y_hgrn = _hgrn(proj, bsz, seq, p)
        y_lru = _lru(proj, bsz, seq, p)
        x2, v, top_idx, top_w, rank, counts = _merge(y_ssd, y_hgrn, y_lru, proj, x2, g1, sc2, sh2, p, seq)
        x2 = _moe(v, top_idx, top_w, rank, counts, x2, g2, fw, p, seq, final_norm=(l == depth - 1))
    return x2.reshape(bsz, seq, d)
```

```python
import functools

import numpy as np
import jax
import jax.numpy as jnp
from jax import lax
from jax.experimental import pallas as pl
from jax.experimental.pallas import tpu as pltpu

F32 = jnp.float32
BF16 = jnp.bfloat16

NORM_EPS = 1e-6
CONV_WIDTH = 4
CHUNK = 64
SSD_CHUNK = 128
SSD_HEADS = 16
SSD_HEAD_DIM = 64
SSD_GROUPS = 2
SSD_STATE = 128
HGRN_HEADS = 8
HGRN_HEAD_DIM = 128
GATE_LOGIT_CLIP = 30.0
LRU_BLOCKS = 4
LRU_C = 8.0
N_EXPERTS = 32
TOP_K = 4
SWIGLU_LIMIT = 7.0
SWIGLU_ALPHA = 1.702
HGRN_LEVELS = (64, 32, 16, 8, 4, 2)

LANE = 128
SUBLANE = 8
VMEM_LIMIT = 48 * 1024 * 1024
ISSUE_UNROLL = 8

SEG = 1024
COL_Z, COL_XS, COL_BCDT, COL_HQ, COL_HF, COL_HI, COL_HG, COL_LG, COL_LX, COL_G0 = range(10)
N_PROJ = 12 * SEG


def _sigmoid(x):
    return jax.nn.sigmoid(x)


def _silu(x):
    return x * _sigmoid(x)


def _softplus(x):
    return jnp.maximum(x, 0.0) + jnp.log1p(jnp.exp(-jnp.abs(x)))


def _log_sigmoid(x):
    return jnp.minimum(x, 0.0) - jnp.log1p(jnp.exp(-jnp.abs(x)))


def _neg_expm1(x):
    t = jnp.tanh(0.5 * x)
    return -2.0 * t / (1.0 - t)


def _gelu_tanh(x):
    return 0.5 * x * (1.0 + jnp.tanh(0.7978845608028654 * (x + 0.044715 * (x * x * x))))


def _split3(x):
    hi = x.astype(BF16)
    r1 = x - hi.astype(F32)
    mid = r1.astype(BF16)
    lo = (r1 - mid.astype(F32)).astype(BF16)
    return hi, mid, lo


def _exact_left_dot(p_bf16, x):
    hi, mid, lo = _split3(x)
    d = lambda a: jnp.dot(p_bf16, a, preferred_element_type=F32)
    return d(hi) + d(mid) + d(lo)


def _dot_nt(a, b):
    return lax.dot_general(a, b, (((1,), (1,)), ((), ())), preferred_element_type=F32)


def _dot_tn(a, b):
    return lax.dot_general(a, b, (((0,), (0,)), ((), ())), preferred_element_type=F32)


def _causal_conv(raw, carry_ref, w_ref, b_ref):
    n = raw.shape[0]
    ext = jnp.concatenate([carry_ref[...], raw], axis=0)
    y = b_ref[...] + w_ref[CONV_WIDTH - 1:CONV_WIDTH, :] * raw
    for k in range(CONV_WIDTH - 1):
        shifted = pltpu.roll(ext, CONV_WIDTH - 1 - k, 0)[SUBLANE:SUBLANE + n]
        y = y + w_ref[k:k + 1, :] * shifted
    carry_ref[...] = raw[n - SUBLANE:n]
    return y


def _adaln_kernel(c_ref, w_ref, b_ref, o_ref):
    c = c_ref[...]
    o_ref[0] = jnp.dot(_silu(c), w_ref[0], precision=lax.Precision.HIGHEST,
                       preferred_element_type=F32) + b_ref[0]


def _adaln(c, ada_w, ada_b):
    depth, d, n = ada_w.shape
    bsz = c.shape[0]
    tn = 1024
    return pl.pallas_call(
        _adaln_kernel,
        out_shape=jax.ShapeDtypeStruct((depth, bsz, n), F32),
        grid=(depth, n // tn),
        in_specs=[pl.BlockSpec((bsz, d), lambda l, j: (0, 0)),
                  pl.BlockSpec((1, d, tn), lambda l, j: (l, 0, j)),
                  pl.BlockSpec((1, 1, tn), lambda l, j: (l, 0, j))],
        out_specs=pl.BlockSpec((1, bsz, tn), lambda l, j: (l, 0, j)),
        compiler_params=pltpu.CompilerParams(dimension_semantics=("arbitrary", "arbitrary")),
        name="adaln",
    )(c, ada_w, ada_b.reshape(depth, 1, n))


def _inproj_kernel(x_ref, sc_ref, sh_ref, g_ref, w_ref, o_ref, u_ref):
    @pl.when(pl.program_id(1) == 0)
    def _():
        x = x_ref[...]
        y = x * lax.rsqrt(jnp.mean(x * x, axis=-1, keepdims=True) + NORM_EPS) * g_ref[...]
        u_ref[...] = (y * (1.0 + sc_ref[0]) + sh_ref[0]).astype(BF16)

    o_ref[...] = jnp.dot(u_ref[...], w_ref[...], preferred_element_type=F32)


def _inproj(x2, sc, sh, g, w, seq):
    t, d = x2.shape
    n = w.shape[1]
    tm = min(1024, seq)
    tn = 1024
    per_b = seq // tm
    return pl.pallas_call(
        _inproj_kernel,
        out_shape=jax.ShapeDtypeStruct((t, n), F32),
        grid=(t // tm, n // tn),
        in_specs=[pl.BlockSpec((tm, d), lambda i, j: (i, 0)),
                  pl.BlockSpec((1, 1, d), lambda i, j: (i // per_b, 0, 0)),
                  pl.BlockSpec((1, 1, d), lambda i, j: (i // per_b, 0, 0)),
                  pl.BlockSpec((1, d), lambda i, j: (0, 0)),
                  pl.BlockSpec((d, tn), lambda i, j: (0, j))],
        out_specs=pl.BlockSpec((tm, tn), lambda i, j: (i, j)),
        scratch_shapes=[pltpu.VMEM((tm, d), BF16)],
        compiler_params=pltpu.CompilerParams(dimension_semantics=("arbitrary", "arbitrary"),
                                             vmem_limit_bytes=VMEM_LIMIT),
        name="inproj",
    )(x2, sc, sh, g, w)


def _ssd_kernel(z_ref, xs_ref, bcdt_ref, cwx_ref, cbx_ref, cwb_ref, cbb_ref, dtb_ref, alog_ref,
                dskip_ref, nw_ref, tri_ref, o_ref, cx_ref, cb_ref, st_ref, xs_s, bc_s, dt_s, y_s):
    ts = z_ref.shape[0]
    p, nst, hpg = SSD_HEAD_DIM, SSD_STATE, SSD_HEADS // SSD_GROUPS

    @pl.when(pl.program_id(1) == 0)
    def _():
        cx_ref[...] = jnp.zeros_like(cx_ref)
        cb_ref[...] = jnp.zeros_like(cb_ref)
        st_ref[...] = jnp.zeros_like(st_ref)

    xs_s[...] = _silu(_causal_conv(xs_ref[...], cx_ref, cwx_ref, cbx_ref))
    bc_s[...] = _silu(_causal_conv(bcdt_ref[:, 0:4 * nst], cb_ref, cwb_ref, cbb_ref))
    dt_s[...] = _softplus(bcdt_ref[:, 4 * nst:4 * nst + LANE] + dtb_ref[...])
    a_neg = -jnp.exp(alog_ref[...])
    lc = SSD_CHUNK
    li = lax.broadcasted_iota(jnp.int32, (lc, lc), 0)
    si = lax.broadcasted_iota(jnp.int32, (lc, lc), 1)
    causal = li >= si

    def chunk_body(c, carry):
        r0 = pl.multiple_of(c * lc, lc)
        rows = pl.ds(r0, lc)
        dt = dt_s[rows, :]
        acs = _exact_left_dot(tri_ref[...], dt * a_neg)
        acs_t = acs.T
        for g in range(SSD_GROUPS):
            b_g = bc_s[rows, g * nst:(g + 1) * nst]
            c_g = bc_s[rows, (SSD_GROUPS + g) * nst:(SSD_GROUPS + g + 1) * nst]
            cbm = _dot_nt(c_g.astype(BF16), b_g.astype(BF16))
            for hh in range(hpg):
                h = g * hpg + hh
                col = acs[:, h:h + 1]
                row = acs_t[h:h + 1, :]
                last = acs[lc - 1:lc, h:h + 1]
                decay = jnp.where(causal, jnp.exp(jnp.where(causal, col - row, 0.0)), 0.0)
                x_h = xs_s[rows, h * p:(h + 1) * p]
                xdt = (x_h * dt[:, h:h + 1]).astype(BF16)
                y = jnp.dot((cbm * decay).astype(BF16), xdt, preferred_element_type=F32)
                state = st_ref[h]
                y = y + _dot_nt((c_g * jnp.exp(col)).astype(BF16), state.astype(BF16))
                bd = (b_g * jnp.exp(last - col)).astype(BF16)
                st_ref[h] = state * jnp.exp(last) + _dot_tn(xdt, bd)
                y_s[rows, h * p:(h + 1) * p] = y + x_h * dskip_ref[:, h * p:(h + 1) * p]
        return carry

    lax.fori_loop(0, ts // lc, chunk_body, 0)

    y = y_s[...] * _silu(z_ref[...])
    half = y.shape[1] // SSD_GROUPS
    outs = []
    for g in range(SSD_GROUPS):
        yg = y[:, g * half:(g + 1) * half]
        outs.append(yg * lax.rsqrt(jnp.mean(yg * yg, axis=-1, keepdims=True) + NORM_EPS))
    o_ref[...] = (jnp.concatenate(outs, axis=1) * nw_ref[...]).astype(o_ref.dtype)


def _ssd(proj, bsz, seq, p):
    t = proj.shape[0]
    ts = min(256, seq)
    per_b = seq // ts
    d = SEG
    row = lambda col: pl.BlockSpec((ts, d), lambda b, s: (b * per_b + s, col))
    const = lambda shape: pl.BlockSpec(shape, lambda b, s: (0,) * len(shape))
    return pl.pallas_call(
        _ssd_kernel,
        out_shape=jax.ShapeDtypeStruct((t, d), BF16),
        grid=(bsz, per_b),
        in_specs=[row(COL_Z), row(COL_XS), row(COL_BCDT),
                  const((CONV_WIDTH, d)), const((1, d)), const((CONV_WIDTH, 4 * SSD_STATE)),
                  const((1, 4 * SSD_STATE)), const((1, LANE)), const((1, LANE)), const((1, d)),
                  const((1, d)), const((SSD_CHUNK, SSD_CHUNK))],
        out_specs=pl.BlockSpec((ts, d), lambda b, s: (b * per_b + s, 0)),
        scratch_shapes=[pltpu.VMEM((SUBLANE, d), F32), pltpu.VMEM((SUBLANE, 4 * SSD_STATE), F32),
                        pltpu.VMEM((SSD_HEADS, SSD_HEAD_DIM, SSD_STATE), F32),
                        pltpu.VMEM((ts, d), F32), pltpu.VMEM((ts, 4 * SSD_STATE), F32),
                        pltpu.VMEM((ts, LANE), F32), pltpu.VMEM((ts, d), F32)],
        compiler_params=pltpu.CompilerParams(dimension_semantics=("arbitrary", "arbitrary"),
                                             vmem_limit_bytes=VMEM_LIMIT),
        name="ssd",
    )(proj, proj, proj, p["cwx"], p["cbx"], p["cwb"], p["cbb"], p["dtb"], p["alog"], p["dskip"],
      p["ssd_norm"], p["tri"])


def _hgrn_tables():
    n = CHUNK
    r = np.arange(n)[:, None]
    j = np.arange(n)[None, :]
    mats = [j <= r, j > r]
    for w in HGRN_LEVELS:
        mid = r - r % w + w // 2
        mats.append(np.where(r >= mid, (j >= mid) & (j <= r), (j > r) & (j < mid)))
    return np.concatenate(mats, axis=0).astype(np.float32)


def _hgrn_kernel(q_ref, f_ref, i_ref, g_ref, lb_ref, nw_ref, pm_ref, o_ref, st_ref, o_s):
    ts = q_ref.shape[0]
    dk = HGRN_HEAD_DIM

    @pl.when(pl.program_id(1) == 0)
    def _():
        st_ref[...] = jnp.zeros_like(st_ref)

    ti = lax.broadcasted_iota(jnp.int32, (CHUNK, CHUNK), 0)
    si = lax.broadcasted_iota(jnp.int32, (CHUNK, CHUNK), 1)
    txs = ti ^ si
    ri = lax.broadcasted_iota(jnp.int32, (CHUNK, 1), 0)

    def chunk_body(c, carry):
        r0 = pl.multiple_of(c * CHUNK, CHUNK)
        rows = pl.ds(r0, CHUNK)
        ff = f_ref[rows, :]
        log_f = _log_sigmoid(ff) + jnp.log1p(lb_ref[...] * jnp.exp(-jnp.maximum(ff, -GATE_LOGIT_CLIP)))
        log_f = jnp.minimum(log_f, 0.0)
        kf = _neg_expm1(log_f)
        qf = _silu(q_ref[rows, :])
        vv = i_ref[rows, :]
        gsum = _exact_left_dot(pm_ref[...], log_f)
        for h in range(HGRN_HEADS):
            hs = slice(h * dk, (h + 1) * dk)
            qh, kh = qf[:, hs], kf[:, hs]
            vh = vv[:, hs].astype(BF16)
            attn = jnp.where(txs == 0, _dot_nt(qh.astype(BF16), kh.astype(BF16)), 0.0)
            for lv, w in enumerate(HGRN_LEVELS):
                e = jnp.exp(gsum[(2 + lv) * CHUNK:(3 + lv) * CHUNK, hs])
                upper = (ri & (w // 2)) != 0
                qt = jnp.where(upper, qh * e, 0.0).astype(BF16)
                kt = jnp.where(upper, 0.0, kh * e).astype(BF16)
                attn = attn + jnp.where(txs < w, _dot_nt(qt, kt), 0.0)
            o = jnp.dot(attn.astype(BF16), vh, preferred_element_type=F32)
            state = st_ref[h]
            qs = (qh * jnp.exp(gsum[0:CHUNK, hs])).astype(BF16)
            o = o + _dot_nt(qs, state.astype(BF16))
            kd = (kh * jnp.exp(gsum[CHUNK:2 * CHUNK, hs])).astype(BF16)
            b_end = gsum[CHUNK - 1:CHUNK, hs]
            st_ref[h] = state * jnp.exp(b_end) + _dot_tn(vh, kd)
            o = o * lax.rsqrt(jnp.mean(o * o, axis=-1, keepdims=True) + NORM_EPS)
            o_s[rows, hs] = o
        return carry

    lax.fori_loop(0, ts // CHUNK, chunk_body, 0, unroll=2)
    o_ref[...] = (o_s[...] * nw_ref[...] * _silu(g_ref[...])).astype(o_ref.dtype)


def _hgrn(proj, bsz, seq, p):
    t = proj.shape[0]
    ts = min(256, seq)
    per_b = seq // ts
    d = SEG
    row = lambda col: pl.BlockSpec((ts, d), lambda b, s: (b * per_b + s, col))
    const = lambda shape: pl.BlockSpec(shape, lambda b, s: (0,) * len(shape))
    n_tab = (2 + len(HGRN_LEVELS)) * CHUNK
    return pl.pallas_call(
        _hgrn_kernel,
        out_shape=jax.ShapeDtypeStruct((t, d), BF16),
        grid=(bsz, per_b),
        in_specs=[row(COL_HQ), row(COL_HF), row(COL_HI), row(COL_HG),
                  const((1, d)), const((1, d)), const((n_tab, CHUNK))],
        out_specs=pl.BlockSpec((ts, d), lambda b, s: (b * per_b + s, 0)),
        scratch_shapes=[pltpu.VMEM((HGRN_HEADS, HGRN_HEAD_DIM, HGRN_HEAD_DIM), F32),
                        pltpu.VMEM((ts, d), F32)],
        compiler_params=pltpu.CompilerParams(dimension_semantics=("arbitrary", "arbitrary"),
                                             vmem_limit_bytes=VMEM_LIMIT),
        name="hgrn2",
    )(proj, proj, proj, proj, p["lb"], p["hgrn_norm"], p["hgrn_tab"])


def _lru_kernel(gate_ref, x_ref, cw_ref, cb_ref, wa_ref, ba_ref, wx_ref, bx_ref, lam_ref,
                o_ref, cx_ref, h_ref):
    ts, d = x_ref.shape
    bd = d // LRU_BLOCKS
    first = pl.program_id(1) == 0

    @pl.when(first)
    def _():
        cx_ref[...] = jnp.zeros_like(cx_ref)
        h_ref[...] = jnp.zeros_like(h_ref)

    xc = _causal_conv(x_ref[...], cx_ref, cw_ref, cb_ref)
    xcb = xc.astype(BF16)
    ra, rx = [], []
    for i in range(LRU_BLOCKS):
        blk = xcb[:, i * bd:(i + 1) * bd]
        ra.append(jnp.dot(blk, wa_ref[i], preferred_element_type=F32))
        rx.append(jnp.dot(blk, wx_ref[i], preferred_element_type=F32))
    r = _sigmoid(jnp.concatenate(ra, axis=1) + ba_ref[...])
    ig = _sigmoid(jnp.concatenate(rx, axis=1) + bx_ref[...])
    log_a = -LRU_C * r * _softplus(-lam_ref[...])
    a = jnp.exp(log_a)
    mult = jnp.sqrt(jnp.maximum(_neg_expm1(2.0 * log_a), 0.0))
    rowi = lax.broadcasted_iota(jnp.int32, (ts, 1), 0)
    mult = jnp.where(jnp.logical_and(first, rowi == 0), 1.0, mult)
    u = mult * ig * xc
    sh = 1
    while sh < ts:
        keep = rowi >= sh
        a_prev = jnp.where(keep, pltpu.roll(a, sh, 0), 1.0)
        u_prev = jnp.where(keep, pltpu.roll(u, sh, 0), 0.0)
        u = a * u_prev + u
        a = a * a_prev
        sh *= 2
    h = u + a * h_ref[0:1, :]
    h_ref[...] = jnp.broadcast_to(h[ts - 1:ts, :], h_ref.shape)
    o_ref[...] = (h * _gelu_tanh(gate_ref[...])).astype(o_ref.dtype)


def _lru(proj, bsz, seq, p):
    t = proj.shape[0]
    ts = min(256, seq)
    per_b = seq // ts
    d = SEG
    bd = d // LRU_BLOCKS
    row = lambda col: pl.BlockSpec((ts, d), lambda b, s: (b * per_b + s, col))
    const = lambda shape: pl.BlockSpec(shape, lambda b, s: (0,) * len(shape))
    return pl.pallas_call(
        _lru_kernel,
        out_shape=jax.ShapeDtypeStruct((t, d), BF16),
        grid=(bsz, per_b),
        in_specs=[row(COL_LG), row(COL_LX), const((CONV_WIDTH, d)), const((1, d)),
                  const((LRU_BLOCKS, bd, bd)), const((1, d)), const((LRU_BLOCKS, bd, bd)), const((1, d)),
                  const((1, d))],
        out_specs=pl.BlockSpec((ts, d), lambda b, s: (b * per_b + s, 0)),
        scratch_shapes=[pltpu.VMEM((SUBLANE, d), F32), pltpu.VMEM((SUBLANE, d), F32)],
        compiler_params=pltpu.CompilerParams(dimension_semantics=("arbitrary", "arbitrary"),
                                             vmem_limit_bytes=VMEM_LIMIT),
        name="rglru",
    )(proj, proj, p["lru_cw"], p["lru_cb"], p["lru_wa"], p["lru_ba"], p["lru_wx"], p["lru_bx"],
      p["lru_lam"])


def _merge_kernel(ys_ref, yh_ref, yl_ref, g0_ref, g1_ref, g2_ref, x_ref, gate_ref, sc_ref, sh_ref,
                  nf_ref, wbs_ref, wbh_ref, wbl_ref, wo_ref, rwt_ref, rb_ref, up_ref,
                  xo_ref, v_ref, idx_ref, tw_ref, rank_ref, cnt_ref, carry_ref):
    tm = x_ref.shape[0]

    @pl.when(pl.program_id(0) == 0)
    def _():
        carry_ref[...] = jnp.zeros_like(carry_ref)

    def branch(y_ref, g_ref, w_ref):
        return _sigmoid(g_ref[...]) * jnp.dot(y_ref[...], w_ref[...], preferred_element_type=F32)

    merged = branch(ys_ref, g0_ref, wbs_ref) + branch(yh_ref, g1_ref, wbh_ref) + branch(yl_ref, g2_ref, wbl_ref)
    mix = jnp.dot(merged.astype(BF16), wo_ref[...], preferred_element_type=F32)
    x = x_ref[...] + gate_ref[0] * mix
    xo_ref[...] = x
    v = x * lax.rsqrt(jnp.mean(x * x, axis=-1, keepdims=True) + NORM_EPS) * nf_ref[...]
    v = v * (1.0 + sc_ref[0]) + sh_ref[0]
    v_ref[...] = v

    v_hi = v.astype(BF16)
    v_lo = (v - v_hi.astype(F32)).astype(BF16)
    logits = (_dot_nt(rwt_ref[0], v_hi) + _dot_nt(rwt_ref[0], v_lo) + _dot_nt(rwt_ref[1], v_hi)
              + rb_ref[...])
    ei = lax.broadcasted_iota(jnp.int32, (N_EXPERTS, tm), 0)
    work = logits
    tops, sel = [], []
    for k in range(TOP_K):
        m = jnp.max(work, axis=0, keepdims=True)
        idx = jnp.min(jnp.where(work == m, ei, N_EXPERTS), axis=0, keepdims=True)
        hit = ei == idx
        tops.append(m)
        sel.append(hit)
        idx_ref[k:k + 1, :] = idx
        work = jnp.where(hit, -jnp.inf, work)
    exps = [jnp.exp(m - tops[0]) for m in tops]
    denom = exps[0] + exps[1] + exps[2] + exps[3]
    for k in range(TOP_K):
        tw_ref[k:k + 1, :] = exps[k] / denom

    member = jnp.where(sel[0] | sel[1] | sel[2] | sel[3], 1.0, 0.0)
    before = jnp.dot(member.astype(BF16), up_ref[...], preferred_element_type=F32) + carry_ref[:, 0:1]
    for k in range(TOP_K):
        rank_ref[k:k + 1, :] = jnp.sum(jnp.where(sel[k], before, 0.0), axis=0, keepdims=True).astype(jnp.int32)
    carry_ref[...] = carry_ref[...] + jnp.sum(member, axis=1, keepdims=True)
    cnt_ref[...] = carry_ref[...]


def _merge(y_ssd, y_hgrn, y_lru, proj, x2, g1, sc2, sh2, p, seq):
    t, d = x2.shape
    tm = min(512, seq)
    per_b = seq // tm
    row = lambda col: pl.BlockSpec((tm, d), lambda i: (i, col))
    mod = pl.BlockSpec((1, 1, d), lambda i: (i // per_b, 0, 0))
    const = lambda shape: pl.BlockSpec(shape, lambda i: (0,) * len(shape))
    tok = lambda: pl.BlockSpec((TOP_K, tm), lambda i: (0, i))
    return pl.pallas_call(
        _merge_kernel,
        out_shape=(jax.ShapeDtypeStruct((t, d), F32), jax.ShapeDtypeStruct((t, d), F32),
                   jax.ShapeDtypeStruct((TOP_K, t), jnp.int32), jax.ShapeDtypeStruct((TOP_K, t), F32),
                   jax.ShapeDtypeStruct((TOP_K, t), jnp.int32), jax.ShapeDtypeStruct((N_EXPERTS, LANE), F32)),
        grid=(t // tm,),
        in_specs=[row(0), row(0), row(0), row(COL_G0), row(COL_G0 + 1), row(COL_G0 + 2), row(0),
                  mod, mod, mod, const((1, d)), const((d, d)), const((d, d)), const((d, d)), const((d, d)),
                  const((2, N_EXPERTS, d)), const((N_EXPERTS, 1)), const((tm, tm))],
        out_specs=(row(0), row(0), tok(), tok(), tok(), const((N_EXPERTS, LANE))),
        scratch_shapes=[pltpu.VMEM((N_EXPERTS, LANE), F32)],
        compiler_params=pltpu.CompilerParams(dimension_semantics=("arbitrary",),
                                             vmem_limit_bytes=VMEM_LIMIT),
        name="merge_router",
    )(y_ssd, y_hgrn, y_lru, proj, proj, proj, x2, g1, sc2, sh2, p["norm_ffn"], p["w_br_ssd"],
      p["w_br_hgrn"], p["w_br_lru"], p["w_out"], p["router_wt"], p["router_b"], p["upper"])


def _dispatch_kernel(dest_hbm, v_ref, xs_in, xs_hbm, idx_s, sem_i, sem_o):
    del xs_in
    tg = v_ref.shape[0]
    i = pl.program_id(0)
    cp = pltpu.make_async_copy(dest_hbm.at[i], idx_s, sem_i)
    cp.start()
    cp.wait()

    def issue(tk, carry):
        for k in range(TOP_K):
            dst = idx_s[k * tg + tk]
            pltpu.make_async_copy(v_ref.at[pl.ds(tk, 1)], xs_hbm.at[pl.ds(dst, 1)], sem_o).start()
        return carry

    lax.fori_loop(0, tg, issue, 0, unroll=ISSUE_UNROLL)
    for k in range(TOP_K):
        pltpu.make_async_copy(v_ref, xs_hbm.at[pl.ds(0, tg)], sem_o).wait()


def _dispatch(dest_tiles, v, xs_zero, tg):
    t, d = v.shape
    n_rows = xs_zero.shape[0]
    return pl.pallas_call(
        _dispatch_kernel,
        out_shape=jax.ShapeDtypeStruct((n_rows, d), F32),
        grid=(t // tg,),
        in_specs=[pl.BlockSpec(memory_space=pl.ANY),
                  pl.BlockSpec((tg, d), lambda i: (i, 0)),
                  pl.BlockSpec(memory_space=pl.ANY)],
        out_specs=pl.BlockSpec(memory_space=pl.ANY),
        scratch_shapes=[pltpu.SMEM((TOP_K * tg,), jnp.int32),
                        pltpu.SemaphoreType.DMA, pltpu.SemaphoreType.DMA],
        input_output_aliases={2: 0},
        compiler_params=pltpu.CompilerParams(dimension_semantics=("arbitrary",)),
        name="moe_dispatch",
    )(dest_tiles, v, xs_zero)


def _expert_kernel(be_ref, nu_ref, x_ref, wgu_ref, bgu_ref, wd_ref, bd_ref, o_ref):
    del be_ref
    i = pl.program_id(0)
    de = wd_ref.shape[1]

    @pl.when(i < nu_ref[0])
    def _():
        gu = jnp.dot(x_ref[...].astype(BF16), wgu_ref[0], preferred_element_type=F32) + bgu_ref[0]
        glu = jnp.minimum(gu[:, :de], SWIGLU_LIMIT)
        lin = jnp.clip(gu[:, de:], -SWIGLU_LIMIT, SWIGLU_LIMIT)
        act = glu * _sigmoid(SWIGLU_ALPHA * glu) * (lin + 1.0)
        o_ref[...] = jnp.dot(act.astype(BF16), wd_ref[0], preferred_element_type=F32) + bd_ref[0]

    @pl.when(i >= nu_ref[0])
    def _():
        o_ref[...] = jnp.zeros_like(o_ref)


def _experts(block_e, n_used, xs, p, blk):
    n_rows, d = xs.shape
    n_blocks = n_rows // blk
    de = p["w_down"].shape[1]
    return pl.pallas_call(
        _expert_kernel,
        out_shape=jax.ShapeDtypeStruct((n_rows, d), F32),
        grid_spec=pltpu.PrefetchScalarGridSpec(
            num_scalar_prefetch=2,
            grid=(n_blocks,),
            in_specs=[pl.BlockSpec((blk, d), lambda i, be, nu: (i, 0)),
                      pl.BlockSpec((1, d, 2 * de), lambda i, be, nu: (be[i], 0, 0)),
                      pl.BlockSpec((1, 1, 2 * de), lambda i, be, nu: (be[i], 0, 0)),
                      pl.BlockSpec((1, de, d), lambda i, be, nu: (be[i], 0, 0)),
                      pl.BlockSpec((1, 1, d), lambda i, be, nu: (be[i], 0, 0))],
            out_specs=pl.BlockSpec((blk, d), lambda i, be, nu: (i, 0))),
        compiler_params=pltpu.CompilerParams(dimension_semantics=("arbitrary",),
                                             vmem_limit_bytes=VMEM_LIMIT),
        name="moe_experts",
    )(block_e, n_used, xs, p["w_gu"], p["b_gu"], p["w_down"], p["b_down"])


def _combine_kernel(dest_hbm, ys_hbm, x_ref, tw_ref, gate_ref, fn_ref, o_ref, idx_s, buf, sem_i, sem_g,
                    *, final_norm):
    tc = x_ref.shape[0]
    i = pl.program_id(0)
    cp = pltpu.make_async_copy(dest_hbm.at[i], idx_s, sem_i)
    cp.start()
    cp.wait()

    def issue(tk, carry):
        for k in range(TOP_K):
            src = idx_s[k * tc + tk]
            pltpu.make_async_copy(ys_hbm.at[pl.ds(src, 1)], buf.at[k, pl.ds(tk, 1)], sem_g).start()
        return carry

    lax.fori_loop(0, tc, issue, 0, unroll=ISSUE_UNROLL)
    for k in range(TOP_K):
        pltpu.make_async_copy(ys_hbm.at[pl.ds(0, tc)], buf.at[k], sem_g).wait()

    moe = tw_ref[:, 0:1] * buf[0]
    for k in range(1, TOP_K):
        moe = moe + tw_ref[:, k:k + 1] * buf[k]
    x = x_ref[...] + gate_ref[0] * moe
    if final_norm:
        x = x * lax.rsqrt(jnp.mean(x * x, axis=-1, keepdims=True) + NORM_EPS) * fn_ref[...]
    o_ref[...] = x


def _combine(dest_tiles, ys, x2, tw_t, g2, final_w, seq, tc, final_norm):
    t, d = x2.shape
    per_b = seq // tc
    return pl.pallas_call(
        functools.partial(_combine_kernel, final_norm=final_norm),
        out_shape=jax.ShapeDtypeStruct((t, d), F32),
        grid=(t // tc,),
        in_specs=[pl.BlockSpec(memory_space=pl.ANY), pl.BlockSpec(memory_space=pl.ANY),
                  pl.BlockSpec((tc, d), lambda i: (i, 0)),
                  pl.BlockSpec((tc, TOP_K), lambda i: (i, 0)),
                  pl.BlockSpec((1, 1, d), lambda i: (i // per_b, 0, 0)),
                  pl.BlockSpec((1, d), lambda i: (0, 0))],
        out_specs=pl.BlockSpec((tc, d), lambda i: (i, 0)),
        scratch_shapes=[pltpu.SMEM((TOP_K * tc,), jnp.int32),
                        pltpu.VMEM((TOP_K, tc, d), F32),
                        pltpu.SemaphoreType.DMA, pltpu.SemaphoreType.DMA],
        compiler_params=pltpu.CompilerParams(dimension_semantics=("arbitrary",),
                                             vmem_limit_bytes=VMEM_LIMIT),
        name="moe_combine",
    )(dest_tiles, ys, x2, tw_t, g2, final_w)


def _moe(v, top_idx, top_w, rank, counts, x2, g2, final_w, p, seq, final_norm):
    t, d = v.shape
    blk = 512 if t * TOP_K >= 512 * N_EXPERTS else 128
    n_blocks = t * TOP_K // blk + N_EXPERTS
    cnt = counts[:, 0].astype(jnp.int32)
    padded = (cnt + blk - 1) // blk * blk
    pend = jnp.cumsum(padded)
    pstart = pend - padded
    dest = rank + jnp.sum(jnp.where(top_idx[None] == jnp.arange(N_EXPERTS, dtype=jnp.int32)[:, None, None],
                                    pstart[:, None, None], 0), axis=0)
    block_start = jnp.arange(n_blocks, dtype=jnp.int32) * blk
    block_e = jnp.minimum(jnp.sum(block_start[:, None] >= pend[None, :], axis=1), N_EXPERTS - 1).astype(jnp.int32)
    n_used = (pend[-1:] // blk).astype(jnp.int32)
    tg = min(512, seq)
    dest_tiles = dest.reshape(TOP_K, t // tg, tg).transpose(1, 0, 2).reshape(t // tg, TOP_K * tg)
    xs = _dispatch(dest_tiles, v, jnp.zeros((n_blocks * blk, d), F32), tg)
    ys = _experts(block_e, n_used, xs, p, blk)
    return _combine(dest_tiles, ys, x2, top_w.T, g2, final_w, seq, tg, final_norm)


def _pack_w_in(w_in):
    d = w_in.shape[0]
    sizes = (1024, 1536, 16, 1024, 1024, 1024, 1024, 1024, 1024, 3072)
    offs = np.concatenate([[0], np.cumsum(sizes)])
    z, xbc, dt, hq, hf, hi, hg, lg, lx, gates = [w_in[:, offs[k]:offs[k + 1]] for k in range(len(sizes))]
    pad = jnp.zeros((d, SEG - 512 - 16), w_in.dtype)
    return jnp.concatenate([z, xbc[:, :1024], xbc[:, 1024:], dt, pad, hq, hf, hi, hg, lg, lx, gates],
                           axis=1).astype(BF16)


def _layer_params(l, a, lower_bound):
    d = a["w_in"].shape[1]
    row = lambda v: v.reshape(1, -1).astype(F32)
    lane_pad = lambda v: jnp.pad(v.astype(F32), (0, LANE - v.shape[0])).reshape(1, LANE)
    r = np.arange(SSD_CHUNK)
    rt = min(512, a["seq"])
    return {
        "w_in": _pack_w_in(a["w_in"][l]),
        "norm_mix": row(a["norm_mix"][l]), "norm_ffn": row(a["norm_ffn"][l]),
        "cwx": a["ssd_conv_w"][l][:, :d], "cbx": row(a["ssd_conv_b"][l][:d]),
        "cwb": a["ssd_conv_w"][l][:, d:], "cbb": row(a["ssd_conv_b"][l][d:]),
        "dtb": lane_pad(a["ssd_dt_bias"][l]), "alog": lane_pad(a["ssd_a_log"][l]),
        "dskip": row(jnp.repeat(a["ssd_d"][l], SSD_HEAD_DIM)), "ssd_norm": row(a["ssd_norm"][l]),
        "tri": jnp.asarray((r[None, :] <= r[:, None]).astype(np.float32), BF16),
        "lb": row(lower_bound), "hgrn_norm": row(a["hgrn_norm"][l]),
        "hgrn_tab": jnp.asarray(_hgrn_tables(), BF16),
        "lru_cw": a["lru_conv_w"][l], "lru_cb": row(a["lru_conv_b"][l]),
        "lru_wa": a["lru_wa"][l].astype(BF16), "lru_ba": row(a["lru_ba"][l]),
        "lru_wx": a["lru_wx"][l].astype(BF16), "lru_bx": row(a["lru_bx"][l]),
        "lru_lam": row(a["lru_lambda"][l]),
        "w_br_ssd": a["w_br_ssd"][l].astype(BF16), "w_br_hgrn": a["w_br_hgrn"][l].astype(BF16),
        "w_br_lru": a["w_br_lru"][l].astype(BF16), "w_out": a["w_out"][l].astype(BF16),
        "router_wt": _router_split(a["router_w"][l]), "router_b": a["router_b"][l].reshape(-1, 1).astype(F32),
        "upper": jnp.asarray((np.arange(rt)[:, None] < np.arange(rt)[None, :]).astype(np.float32), BF16),
        "w_gu": a["moe_w_gu"][l].astype(BF16), "b_gu": a["moe_b_gu"][l][:, None, :],
        "w_down": a["moe_w_down"][l].astype(BF16), "b_down": a["moe_b_down"][l][:, None, :],
    }


def _router_split(w):
    wt = w.T.astype(F32)
    hi = wt.astype(BF16)
    lo = (wt - hi.astype(F32)).astype(BF16)
    return jnp.stack([hi, lo])


def kernel(x, c, ada_w, ada_b, norm_mix, norm_ffn, w_in, ssd_conv_w, ssd_conv_b, ssd_dt_bias, ssd_a_log, ssd_d, ssd_norm, hgrn_lb, hgrn_norm, lru_conv_w, lru_conv_b, lru_wa, lru_ba, lru_wx, lru_bx, lru_lambda, w_br_ssd, w_br_hgrn, w_br_lru, w_out, router_w, router_b, moe_w_gu, moe_b_gu, moe_w_down, moe_b_down, final_norm):
    bsz, seq, d = x.shape
    depth = ada_w.shape[0]
    a = dict(seq=seq, norm_mix=norm_mix, norm_ffn=norm_ffn, w_in=w_in, ssd_conv_w=ssd_conv_w,
             ssd_conv_b=ssd_conv_b, ssd_dt_bias=ssd_dt_bias, ssd_a_log=ssd_a_log, ssd_d=ssd_d,
             ssd_norm=ssd_norm, hgrn_norm=hgrn_norm, lru_conv_w=lru_conv_w, lru_conv_b=lru_conv_b,
             lru_wa=lru_wa, lru_ba=lru_ba, lru_wx=lru_wx, lru_bx=lru_bx, lru_lambda=lru_lambda,
             w_br_ssd=w_br_ssd, w_br_hgrn=w_br_hgrn, w_br_lru=w_br_lru, w_out=w_out, router_w=router_w,
             router_b=router_b, moe_w_gu=moe_w_gu, moe_b_gu=moe_b_gu, moe_w_down=moe_w_down,
             moe_b_down=moe_b_down)
    lb_soft = jax.nn.softmax(hgrn_lb.astype(F32), axis=0)
    lower_bounds = jnp.cumsum(lb_soft, axis=0) - lb_soft[0]
    mod = _adaln(c, ada_w, ada_b).reshape(depth, bsz, 6, 1, d)
    x2 = x.reshape(bsz * seq, d)
    fw = final_norm.reshape(1, d)
    for l in range(depth):
        p = _layer_params(l, a, lower_bounds[l])
        sh1, sc1, g1, sh2, sc2, g2 = [mod[l, :, k] for k in range(6)]
        proj = _inproj(x2, sc1, sh1, p["norm_mix"], p["w_in"], seq)
        y_ssd = _ssd(proj, bsz, seq, p)
        y_hgrn = _hgrn(proj, bsz, seq, p)
        y_lru = _lru(proj, bsz, seq, p)
        x2, v, top_idx, top_w, rank, counts = _merge(y_ssd, y_hgrn, y_lru, proj, x2, g1, sc2, sh2, p, seq)
        x2 = _moe(v, top_idx, top_w, rank, counts, x2, g2, fw, p, seq, final_norm=(l == depth - 1))
    return x2.reshape(bsz, seq, d)
```

```python
import functools

import numpy as np
import jax
import jax.numpy as jnp
from jax import lax
from jax.experimental import pallas as pl
from jax.experimental.pallas import tpu as pltpu

F32 = jnp.float32
BF16 = jnp.bfloat16

NORM_EPS = 1e-6
CONV_WIDTH = 4
CHUNK = 64
SSD_CHUNK = 128
SSD_HEADS = 16
SSD_HEAD_DIM = 64
SSD_GROUPS = 2
SSD_STATE = 128
HGRN_HEADS = 8
HGRN_HEAD_DIM = 128
GATE_LOGIT_CLIP = 30.0
LRU_BLOCKS = 4
LRU_C = 8.0
LRU_SCAN_SUB = 32
N_EXPERTS = 32
TOP_K = 4
SWIGLU_LIMIT = 7.0
SWIGLU_ALPHA = 1.702
HGRN_LEVELS = (64, 32, 16, 8, 4, 2)

LANE = 128
SUBLANE = 8
VMEM_LIMIT = 48 * 1024 * 1024
ISSUE_UNROLL = 8

SEG = 1024
COL_Z, COL_XS, COL_BCDT, COL_HQ, COL_HF, COL_HI, COL_HG, COL_LG, COL_LX, COL_G0 = range(10)
N_PROJ = 12 * SEG


def _sigmoid(x):
    return jax.nn.sigmoid(x)


def _silu(x):
    return x * _sigmoid(x)


def _softplus(x):
    return jnp.maximum(x, 0.0) + jnp.log1p(jnp.exp(-jnp.abs(x)))


def _log_sigmoid(x):
    return jnp.minimum(x, 0.0) - jnp.log1p(jnp.exp(-jnp.abs(x)))


def _neg_expm1(x):
    t = jnp.tanh(0.5 * x)
    return -2.0 * t / (1.0 - t)


def _gelu_tanh(x):
    return 0.5 * x * (1.0 + jnp.tanh(0.7978845608028654 * (x + 0.044715 * (x * x * x))))


def _split3(x):
    hi = x.astype(BF16)
    r1 = x - hi.astype(F32)
    mid = r1.astype(BF16)
    lo = (r1 - mid.astype(F32)).astype(BF16)
    return hi, mid, lo


def _exact_left_dot(p_bf16, x):
    hi, mid, lo = _split3(x)
    d = lambda a: jnp.dot(p_bf16, a, preferred_element_type=F32)
    return d(hi) + d(mid) + d(lo)


def _dot_nt(a, b):
    return lax.dot_general(a, b, (((1,), (1,)), ((), ())), preferred_element_type=F32)


def _dot_tn(a, b):
    return lax.dot_general(a, b, (((0,), (0,)), ((), ())), preferred_element_type=F32)


def _causal_conv(raw, carry_ref, w_ref, b_ref):
    n = raw.shape[0]
    ext = jnp.concatenate([carry_ref[...], raw], axis=0)
    y = b_ref[...] + w_ref[CONV_WIDTH - 1:CONV_WIDTH, :] * raw
    for k in range(CONV_WIDTH - 1):
        shifted = pltpu.roll(ext, CONV_WIDTH - 1 - k, 0)[SUBLANE:SUBLANE + n]
        y = y + w_ref[k:k + 1, :] * shifted
    carry_ref[...] = raw[n - SUBLANE:n]
    return y


def _adaln_kernel(c_ref, w_ref, b_ref, o_ref):
    c = c_ref[...]
    o_ref[0] = jnp.dot(_silu(c), w_ref[0], precision=lax.Precision.HIGHEST,
                       preferred_element_type=F32) + b_ref[0]


def _adaln(c, ada_w, ada_b):
    depth, d, n = ada_w.shape
    bsz = c.shape[0]
    tn = 1024
    return pl.pallas_call(
        _adaln_kernel,
        out_shape=jax.ShapeDtypeStruct((depth, bsz, n), F32),
        grid=(depth, n // tn),
        in_specs=[pl.BlockSpec((bsz, d), lambda l, j: (0, 0)),
                  pl.BlockSpec((1, d, tn), lambda l, j: (l, 0, j)),
                  pl.BlockSpec((1, 1, tn), lambda l, j: (l, 0, j))],
        out_specs=pl.BlockSpec((1, bsz, tn), lambda l, j: (l, 0, j)),
        compiler_params=pltpu.CompilerParams(dimension_semantics=("arbitrary", "arbitrary")),
        name="adaln",
    )(c, ada_w, ada_b.reshape(depth, 1, n))


def _inproj_kernel(x_ref, sc_ref, sh_ref, g_ref, w_ref, o_ref, u_ref):
    @pl.when(pl.program_id(1) == 0)
    def _():
        x = x_ref[...]
        y = x * lax.rsqrt(jnp.mean(x * x, axis=-1, keepdims=True) + NORM_EPS) * g_ref[...]
        u_ref[...] = (y * (1.0 + sc_ref[0]) + sh_ref[0]).astype(BF16)

    o_ref[...] = jnp.dot(u_ref[...], w_ref[...], preferred_element_type=F32)


def _inproj(x2, sc, sh, g, w, seq):
    t, d = x2.shape
    n = w.shape[1]
    tm = min(1024, seq)
    tn = 2048
    per_b = seq // tm
    return pl.pallas_call(
        _inproj_kernel,
        out_shape=jax.ShapeDtypeStruct((t, n), F32),
        grid=(t // tm, n // tn),
        in_specs=[pl.BlockSpec((tm, d), lambda i, j: (i, 0)),
                  pl.BlockSpec((1, 1, d), lambda i, j: (i // per_b, 0, 0)),
                  pl.BlockSpec((1, 1, d), lambda i, j: (i // per_b, 0, 0)),
                  pl.BlockSpec((1, d), lambda i, j: (0, 0)),
                  pl.BlockSpec((d, tn), lambda i, j: (0, j))],
        out_specs=pl.BlockSpec((tm, tn), lambda i, j: (i, j)),
        scratch_shapes=[pltpu.VMEM((tm, d), BF16)],
        compiler_params=pltpu.CompilerParams(dimension_semantics=("arbitrary", "arbitrary"),
                                             vmem_limit_bytes=VMEM_LIMIT),
        name="inproj",
    )(x2, sc, sh, g, w)


def _ssd_kernel(z_ref, xs_ref, bcdt_ref, cwx_ref, cbx_ref, cwb_ref, cbb_ref, dtb_ref, alog_ref,
                dskip_ref, nw_ref, tri_ref, ex_ref, o_ref, cx_ref, cb_ref, st_ref, xs_s, bc_s, dt_s, y_s):
    ts = z_ref.shape[0]
    p, nst, hpg = SSD_HEAD_DIM, SSD_STATE, SSD_HEADS // SSD_GROUPS

    @pl.when(pl.program_id(1) == 0)
    def _():
        cx_ref[...] = jnp.zeros_like(cx_ref)
        cb_ref[...] = jnp.zeros_like(cb_ref)
        st_ref[...] = jnp.zeros_like(st_ref)

    xs_s[...] = _silu(_causal_conv(xs_ref[...], cx_ref, cwx_ref, cbx_ref))
    bc_s[...] = _silu(_causal_conv(bcdt_ref[:, 0:4 * nst], cb_ref, cwb_ref, cbb_ref))
    dt_s[...] = _softplus(bcdt_ref[:, 4 * nst:4 * nst + LANE] + dtb_ref[...])
    a_neg = -jnp.exp(alog_ref[...])
    lc = SSD_CHUNK
    li = lax.broadcasted_iota(jnp.int32, (lc, lc), 0)
    si = lax.broadcasted_iota(jnp.int32, (lc, lc), 1)
    causal = li >= si
    first_half = lax.broadcasted_iota(jnp.int32, (lc, 2 * p), 1) < p

    def chunk_body(c, carry):
        r0 = pl.multiple_of(c * lc, lc)
        rows = pl.ds(r0, lc)
        dt = dt_s[rows, :]
        acs = _exact_left_dot(tri_ref[...], dt * a_neg)
        acs_t = acs.T
        pieces = jnp.concatenate(list(_split3(acs)) + list(_split3(dt)), axis=0)
        spread = jnp.dot(pieces, ex_ref[...], preferred_element_type=F32)
        acs_x = spread[0:lc] + spread[lc:2 * lc] + spread[2 * lc:3 * lc]
        dt_x = spread[3 * lc:4 * lc] + spread[4 * lc:5 * lc] + spread[5 * lc:6 * lc]
        xs_c = xs_s[rows, :]
        xdt = xs_c * dt_x
        xdte = (xdt * jnp.exp(acs_x[lc - 1:lc, :] - acs_x)).astype(BF16)
        grow = jnp.exp(acs_x)
        for g in range(SSD_GROUPS):
            b_g = bc_s[rows, g * nst:(g + 1) * nst].astype(BF16)
            c_g = bc_s[rows, (SSD_GROUPS + g) * nst:(SSD_GROUPS + g + 1) * nst].astype(BF16)
            cbm = _dot_nt(c_g, b_g)
            prev = st_ref[g * hpg:(g + 1) * hpg].reshape(hpg * p, nst)
            y_off = _dot_nt(c_g, prev.astype(BF16))
            for pr in range(hpg // 2):
                ha = g * hpg + 2 * pr
                pair = slice(ha * p, (ha + 2) * p)
                xp = xdt[:, pair]
                yd = None
                for k in range(2):
                    h = ha + k
                    col = acs[:, h:h + 1]
                    row = acs_t[h:h + 1, :]
                    decay = jnp.where(causal, jnp.exp(jnp.where(causal, col - row, 0.0)), 0.0)
                    x_k = jnp.where(first_half if k == 0 else ~first_half, xp, 0.0).astype(BF16)
                    part = jnp.dot((cbm * decay).astype(BF16), x_k, preferred_element_type=F32)
                    yd = part if yd is None else yd + part
                lo = 2 * pr * p
                y_s[rows, pair] = (yd + grow[:, pair] * y_off[:, lo:lo + 2 * p]
                                   + xs_c[:, pair] * dskip_ref[:, pair])
                upd = _dot_tn(xdte[:, pair], b_g)
                for k in range(2):
                    h = ha + k
                    st_ref[h] = st_ref[h] * jnp.exp(acs[lc - 1:lc, h:h + 1]) + upd[k * p:(k + 1) * p]
        return carry

    lax.fori_loop(0, ts // lc, chunk_body, 0)

    y = y_s[...] * _silu(z_ref[...])
    half = y.shape[1] // SSD_GROUPS
    outs = []
    for g in range(SSD_GROUPS):
        yg = y[:, g * half:(g + 1) * half]
        outs.append(yg * lax.rsqrt(jnp.mean(yg * yg, axis=-1, keepdims=True) + NORM_EPS))
    o_ref[...] = (jnp.concatenate(outs, axis=1) * nw_ref[...]).astype(o_ref.dtype)


def _ssd(proj, bsz, seq, p):
    t = proj.shape[0]
    ts = min(256, seq)
    per_b = seq // ts
    d = SEG
    row = lambda col: pl.BlockSpec((ts, d), lambda b, s: (b * per_b + s, col))
    const = lambda shape: pl.BlockSpec(shape, lambda b, s: (0,) * len(shape))
    return pl.pallas_call(
        _ssd_kernel,
        out_shape=jax.ShapeDtypeStruct((t, d), BF16),
        grid=(bsz, per_b),
        in_specs=[row(COL_Z), row(COL_XS), row(COL_BCDT),
                  const((CONV_WIDTH, d)), const((1, d)), const((CONV_WIDTH, 4 * SSD_STATE)),
                  const((1, 4 * SSD_STATE)), const((1, LANE)), const((1, LANE)), const((1, d)),
                  const((1, d)), const((SSD_CHUNK, SSD_CHUNK)), const((LANE, d))],
        out_specs=pl.BlockSpec((ts, d), lambda b, s: (b * per_b + s, 0)),
        scratch_shapes=[pltpu.VMEM((SUBLANE, d), F32), pltpu.VMEM((SUBLANE, 4 * SSD_STATE), F32),
                        pltpu.VMEM((SSD_HEADS, SSD_HEAD_DIM, SSD_STATE), F32),
                        pltpu.VMEM((ts, d), F32), pltpu.VMEM((ts, 4 * SSD_STATE), F32),
                        pltpu.VMEM((ts, LANE), F32), pltpu.VMEM((ts, d), F32)],
        compiler_params=pltpu.CompilerParams(dimension_semantics=("arbitrary", "arbitrary"),
                                             vmem_limit_bytes=VMEM_LIMIT),
        name="ssd",
    )(proj, proj, proj, p["cwx"], p["cbx"], p["cwb"], p["cbb"], p["dtb"], p["alog"], p["dskip"],
      p["ssd_norm"], p["tri"], p["ex"])


def _hgrn_tables():
    n = CHUNK
    r = np.arange(n)[:, None]
    j = np.arange(n)[None, :]
    mats = [j <= r, j > r]
    for w in HGRN_LEVELS:
        mid = r - r % w + w // 2
        mats.append(np.where(r >= mid, (j >= mid) & (j <= r), (j > r) & (j < mid)))
    return np.concatenate(mats, axis=0).astype(np.float32)


def _hgrn_kernel(q_ref, f_ref, i_ref, g_ref, lb_ref, nw_ref, pm_ref, o_ref, st_ref, o_s):
    ts = q_ref.shape[0]
    dk = HGRN_HEAD_DIM

    @pl.when(pl.program_id(1) == 0)
    def _():
        st_ref[...] = jnp.zeros_like(st_ref)

    ti = lax.broadcasted_iota(jnp.int32, (CHUNK, CHUNK), 0)
    si = lax.broadcasted_iota(jnp.int32, (CHUNK, CHUNK), 1)
    txs = ti ^ si
    ri = lax.broadcasted_iota(jnp.int32, (CHUNK, 1), 0)

    def chunk_body(c, carry):
        r0 = pl.multiple_of(c * CHUNK, CHUNK)
        rows = pl.ds(r0, CHUNK)
        ff = f_ref[rows, :]
        log_f = _log_sigmoid(ff) + jnp.log1p(lb_ref[...] * jnp.exp(-jnp.maximum(ff, -GATE_LOGIT_CLIP)))
        log_f = jnp.minimum(log_f, 0.0)
        kf = 1.0 - jnp.exp(log_f)
        qf = _silu(q_ref[rows, :])
        vv = i_ref[rows, :]
        gsum = _exact_left_dot(pm_ref[...], log_f)
        for h in range(HGRN_HEADS):
            hs = slice(h * dk, (h + 1) * dk)
            qh, kh = qf[:, hs], kf[:, hs]
            vh = vv[:, hs].astype(BF16)
            attn = jnp.where(txs == 0, _dot_nt(qh.astype(BF16), kh.astype(BF16)), 0.0)
            for lv, w in enumerate(HGRN_LEVELS):
                e = jnp.exp(gsum[(2 + lv) * CHUNK:(3 + lv) * CHUNK, hs])
                upper = (ri & (w // 2)) != 0
                qt = jnp.where(upper, qh * e, 0.0).astype(BF16)
                kt = jnp.where(upper, 0.0, kh * e).astype(BF16)
                attn = attn + jnp.where(txs < w, _dot_nt(qt, kt), 0.0)
            o = jnp.dot(attn.astype(BF16), vh, preferred_element_type=F32)
            state = st_ref[h]
            qs = (qh * jnp.exp(gsum[0:CHUNK, hs])).astype(BF16)
            o = o + _dot_nt(qs, state.astype(BF16))
            kd = (kh * jnp.exp(gsum[CHUNK:2 * CHUNK, hs])).astype(BF16)
            b_end = gsum[CHUNK - 1:CHUNK, hs]
            st_ref[h] = state * jnp.exp(b_end) + _dot_tn(vh, kd)
            o = o * lax.rsqrt(jnp.mean(o * o, axis=-1, keepdims=True) + NORM_EPS)
            o_s[rows, hs] = o
        return carry

    lax.fori_loop(0, ts // CHUNK, chunk_body, 0, unroll=2)
    o_ref[...] = (o_s[...] * nw_ref[...] * _silu(g_ref[...])).astype(o_ref.dtype)


def _hgrn(proj, bsz, seq, p):
    t = proj.shape[0]
    ts = min(256, seq)
    per_b = seq // ts
    d = SEG
    row = lambda col: pl.BlockSpec((ts, d), lambda b, s: (b * per_b + s, col))
    const = lambda shape: pl.BlockSpec(shape, lambda b, s: (0,) * len(shape))
    n_tab = (2 + len(HGRN_LEVELS)) * CHUNK
    return pl.pallas_call(
        _hgrn_kernel,
        out_shape=jax.ShapeDtypeStruct((t, d), BF16),
        grid=(bsz, per_b),
        in_specs=[row(COL_HQ), row(COL_HF), row(COL_HI), row(COL_HG),
                  const((1, d)), const((1, d)), const((n_tab, CHUNK))],
        out_specs=pl.BlockSpec((ts, d), lambda b, s: (b * per_b + s, 0)),
        scratch_shapes=[pltpu.VMEM((HGRN_HEADS, HGRN_HEAD_DIM, HGRN_HEAD_DIM), F32),
                        pltpu.VMEM((ts, d), F32)],
        compiler_params=pltpu.CompilerParams(dimension_semantics=("arbitrary", "arbitrary"),
                                             vmem_limit_bytes=VMEM_LIMIT),
        name="hgrn2",
    )(proj, proj, proj, proj, p["lb"], p["hgrn_norm"], p["hgrn_tab"])


def _lru_kernel(gate_ref, x_ref, cw_ref, cb_ref, wa_ref, ba_ref, wx_ref, bx_ref, lam_ref,
                o_ref, cx_ref, h_ref):
    ts, d = x_ref.shape
    bd = d // LRU_BLOCKS
    first = pl.program_id(1) == 0

    @pl.when(first)
    def _():
        cx_ref[...] = jnp.zeros_like(cx_ref)
        h_ref[...] = jnp.zeros_like(h_ref)

    xc = _causal_conv(x_ref[...], cx_ref, cw_ref, cb_ref)
    xcb = xc.astype(BF16)
    ra, rx = [], []
    for i in range(LRU_BLOCKS):
        blk = xcb[:, i * bd:(i + 1) * bd]
        ra.append(jnp.dot(blk, wa_ref[i], preferred_element_type=F32))
        rx.append(jnp.dot(blk, wx_ref[i], preferred_element_type=F32))
    r = _sigmoid(jnp.concatenate(ra, axis=1) + ba_ref[...])
    ig = _sigmoid(jnp.concatenate(rx, axis=1) + bx_ref[...])
    log_a = -LRU_C * r * _softplus(-lam_ref[...])
    a = jnp.exp(log_a)
    mult = jnp.sqrt(jnp.maximum(_neg_expm1(2.0 * log_a), 0.0))
    rowi = lax.broadcasted_iota(jnp.int32, (ts, 1), 0)
    mult = jnp.where(jnp.logical_and(first, rowi == 0), 1.0, mult)
    u = mult * ig * xc
    sub = min(LRU_SCAN_SUB, ts)
    rmod = rowi & (sub - 1)
    sh = 1
    while sh < sub:
        keep = rmod >= sh
        a_prev = jnp.where(keep, pltpu.roll(a, sh, 0), 1.0)
        u_prev = jnp.where(keep, pltpu.roll(u, sh, 0), 0.0)
        u = a * u_prev + u
        a = a * a_prev
        sh *= 2
    carry = h_ref[0:1, :]
    parts = []
    for j in range(ts // sub):
        hj = u[j * sub:(j + 1) * sub] + a[j * sub:(j + 1) * sub] * carry
        carry = hj[sub - 1:sub]
        parts.append(hj)
    h = jnp.concatenate(parts, axis=0)
    h_ref[...] = jnp.broadcast_to(h[ts - 1:ts, :], h_ref.shape)
    o_ref[...] = (h * _gelu_tanh(gate_ref[...])).astype(o_ref.dtype)


def _lru(proj, bsz, seq, p):
    t = proj.shape[0]
    ts = min(256, seq)
    per_b = seq // ts
    d = SEG
    bd = d // LRU_BLOCKS
    row = lambda col: pl.BlockSpec((ts, d), lambda b, s: (b * per_b + s, col))
    const = lambda shape: pl.BlockSpec(shape, lambda b, s: (0,) * len(shape))
    return pl.pallas_call(
        _lru_kernel,
        out_shape=jax.ShapeDtypeStruct((t, d), BF16),
        grid=(bsz, per_b),
        in_specs=[row(COL_LG), row(COL_LX), const((CONV_WIDTH, d)), const((1, d)),
                  const((LRU_BLOCKS, bd, bd)), const((1, d)), const((LRU_BLOCKS, bd, bd)), const((1, d)),
                  const((1, d))],
        out_specs=pl.BlockSpec((ts, d), lambda b, s: (b * per_b + s, 0)),
        scratch_shapes=[pltpu.VMEM((SUBLANE, d), F32), pltpu.VMEM((SUBLANE, d), F32)],
        compiler_params=pltpu.CompilerParams(dimension_semantics=("arbitrary", "arbitrary"),
                                             vmem_limit_bytes=VMEM_LIMIT),
        name="rglru",
    )(proj, proj, p["lru_cw"], p["lru_cb"], p["lru_wa"], p["lru_ba"], p["lru_wx"], p["lru_bx"],
      p["lru_lam"])


def _merge_kernel(ys_ref, yh_ref, yl_ref, g0_ref, g1_ref, g2_ref, x_ref, gate_ref, sc_ref, sh_ref,
                  nf_ref, wbs_ref, wbh_ref, wbl_ref, wo_ref, rwt_ref, rb_ref, up_ref,
                  xo_ref, v_ref, idx_ref, tw_ref, rank_ref, cnt_ref, carry_ref):
    tm = x_ref.shape[0]

    @pl.when(pl.program_id(0) == 0)
    def _():
        carry_ref[...] = jnp.zeros_like(carry_ref)

    def branch(y_ref, g_ref, w_ref):
        return _sigmoid(g_ref[...]) * jnp.dot(y_ref[...], w_ref[...], preferred_element_type=F32)

    merged = branch(ys_ref, g0_ref, wbs_ref) + branch(yh_ref, g1_ref, wbh_ref) + branch(yl_ref, g2_ref, wbl_ref)
    mix = jnp.dot(merged.astype(BF16), wo_ref[...], preferred_element_type=F32)
    x = x_ref[...] + gate_ref[0] * mix
    xo_ref[...] = x
    v = x * lax.rsqrt(jnp.mean(x * x, axis=-1, keepdims=True) + NORM_EPS) * nf_ref[...]
    v = v * (1.0 + sc_ref[0]) + sh_ref[0]
    v_ref[...] = v

    v_hi = v.astype(BF16)
    v_lo = (v - v_hi.astype(F32)).astype(BF16)
    logits = (_dot_nt(rwt_ref[0], v_hi) + _dot_nt(rwt_ref[0], v_lo) + _dot_nt(rwt_ref[1], v_hi)
              + rb_ref[...])
    ei = lax.broadcasted_iota(jnp.int32, (N_EXPERTS, tm), 0)
    work = logits
    tops, sel = [], []
    for k in range(TOP_K):
        m = jnp.max(work, axis=0, keepdims=True)
        idx = jnp.min(jnp.where(work == m, ei, N_EXPERTS), axis=0, keepdims=True)
        hit = ei == idx
        tops.append(m)
        sel.append(hit)
        idx_ref[k:k + 1, :] = idx
        work = jnp.where(hit, -jnp.inf, work)
    exps = [jnp.exp(m - tops[0]) for m in tops]
    denom = exps[0] + exps[1] + exps[2] + exps[3]
    for k in range(TOP_K):
        tw_ref[k:k + 1, :] = exps[k] / denom

    member = jnp.where(sel[0] | sel[1] | sel[2] | sel[3], 1.0, 0.0)
    before = jnp.dot(member.astype(BF16), up_ref[...], preferred_element_type=F32) + carry_ref[:, 0:1]
    for k in range(TOP_K):
        rank_ref[k:k + 1, :] = jnp.sum(jnp.where(sel[k], before, 0.0), axis=0, keepdims=True).astype(jnp.int32)
    carry_ref[...] = carry_ref[...] + jnp.sum(member, axis=1, keepdims=True)
    cnt_ref[...] = carry_ref[...]


def _merge(y_ssd, y_hgrn, y_lru, proj, x2, g1, sc2, sh2, p, seq):
    t, d = x2.shape
    tm = min(512, seq)
    per_b = seq // tm
    row = lambda col: pl.BlockSpec((tm, d), lambda i: (i, col))
    mod = pl.BlockSpec((1, 1, d), lambda i: (i // per_b, 0, 0))
    const = lambda shape: pl.BlockSpec(shape, lambda i: (0,) * len(shape))
    tok = lambda: pl.BlockSpec((TOP_K, tm), lambda i: (0, i))
    return pl.pallas_call(
        _merge_kernel,
        out_shape=(jax.ShapeDtypeStruct((t, d), F32), jax.ShapeDtypeStruct((t, d), F32),
                   jax.ShapeDtypeStruct((TOP_K, t), jnp.int32), jax.ShapeDtypeStruct((TOP_K, t), F32),
                   jax.ShapeDtypeStruct((TOP_K, t), jnp.int32), jax.ShapeDtypeStruct((N_EXPERTS, LANE), F32)),
        grid=(t // tm,),
        in_specs=[row(0), row(0), row(0), row(COL_G0), row(COL_G0 + 1), row(COL_G0 + 2), row(0),
                  mod, mod, mod, const((1, d)), const((d, d)), const((d, d)), const((d, d)), const((d, d)),
                  const((2, N_EXPERTS, d)), const((N_EXPERTS, 1)), const((tm, tm))],
        out_specs=(row(0), row(0), tok(), tok(), tok(), const((N_EXPERTS, LANE))),
        scratch_shapes=[pltpu.VMEM((N_EXPERTS, LANE), F32)],
        compiler_params=pltpu.CompilerParams(dimension_semantics=("arbitrary",),
                                             vmem_limit_bytes=VMEM_LIMIT),
        name="merge_router",
    )(y_ssd, y_hgrn, y_lru, proj, proj, proj, x2, g1, sc2, sh2, p["norm_ffn"], p["w_br_ssd"],
      p["w_br_hgrn"], p["w_br_lru"], p["w_out"], p["router_wt"], p["router_b"], p["upper"])


def _dispatch_kernel(dest_hbm, v_ref, xs_in, xs_hbm, idx_s, sem_i, sem_o):
    del xs_in
    tg = v_ref.shape[0]
    i = pl.program_id(0)
    cp = pltpu.make_async_copy(dest_hbm.at[i], idx_s, sem_i)
    cp.start()
    cp.wait()

    def issue(tk, carry):
        for k in range(TOP_K):
            dst = idx_s[k * tg + tk]
            pltpu.make_async_copy(v_ref.at[pl.ds(tk, 1)], xs_hbm.at[pl.ds(dst, 1)], sem_o).start()
        return carry

    lax.fori_loop(0, tg, issue, 0, unroll=ISSUE_UNROLL)
    for k in range(TOP_K):
        pltpu.make_async_copy(v_ref, xs_hbm.at[pl.ds(0, tg)], sem_o).wait()


def _dispatch(dest_tiles, v, xs_zero, tg):
    t, d = v.shape
    n_rows = xs_zero.shape[0]
    return pl.pallas_call(
        _dispatch_kernel,
        out_shape=jax.ShapeDtypeStruct((n_rows, d), F32),
        grid=(t // tg,),
        in_specs=[pl.BlockSpec(memory_space=pl.ANY),
                  pl.BlockSpec((tg, d), lambda i: (i, 0)),
                  pl.BlockSpec(memory_space=pl.ANY)],
        out_specs=pl.BlockSpec(memory_space=pl.ANY),
        scratch_shapes=[pltpu.SMEM((TOP_K * tg,), jnp.int32),
                        pltpu.SemaphoreType.DMA, pltpu.SemaphoreType.DMA],
        input_output_aliases={2: 0},
        compiler_params=pltpu.CompilerParams(dimension_semantics=("arbitrary",)),
        name="moe_dispatch",
    )(dest_tiles, v, xs_zero)


def _expert_kernel(be_ref, nu_ref, x_ref, wgu_ref, bgu_ref, wd_ref, bd_ref, o_ref):
    del be_ref
    i = pl.program_id(0)
    de = wd_ref.shape[1]

    @pl.when(i < nu_ref[0])
    def _():
        gu = jnp.dot(x_ref[...].astype(BF16), wgu_ref[0], preferred_element_type=F32) + bgu_ref[0]
        glu = jnp.minimum(gu[:, :de], SWIGLU_LIMIT)
        lin = jnp.clip(gu[:, de:], -SWIGLU_LIMIT, SWIGLU_LIMIT)
        act = glu * _sigmoid(SWIGLU_ALPHA * glu) * (lin + 1.0)
        o_ref[...] = jnp.dot(act.astype(BF16), wd_ref[0], preferred_element_type=F32) + bd_ref[0]

    @pl.when(i >= nu_ref[0])
    def _():
        o_ref[...] = jnp.zeros_like(o_ref)


def _experts(block_e, n_used, xs, p, blk):
    n_rows, d = xs.shape
    n_blocks = n_rows // blk
    de = p["w_down"].shape[1]
    return pl.pallas_call(
        _expert_kernel,
        out_shape=jax.ShapeDtypeStruct((n_rows, d), F32),
        grid_spec=pltpu.PrefetchScalarGridSpec(
            num_scalar_prefetch=2,
            grid=(n_blocks,),
            in_specs=[pl.BlockSpec((blk, d), lambda i, be, nu: (i, 0)),
                      pl.BlockSpec((1, d, 2 * de), lambda i, be, nu: (be[i], 0, 0)),
                      pl.BlockSpec((1, 1, 2 * de), lambda i, be, nu: (be[i], 0, 0)),
                      pl.BlockSpec((1, de, d), lambda i, be, nu: (be[i], 0, 0)),
                      pl.BlockSpec((1, 1, d), lambda i, be, nu: (be[i], 0, 0))],
            out_specs=pl.BlockSpec((blk, d), lambda i, be, nu: (i, 0))),
        compiler_params=pltpu.CompilerParams(dimension_semantics=("arbitrary",),
                                             vmem_limit_bytes=VMEM_LIMIT),
        name="moe_experts",
    )(block_e, n_used, xs, p["w_gu"], p["b_gu"], p["w_down"], p["b_down"])


def _combine_kernel(dest_hbm, ys_hbm, x_ref, tw_ref, gate_ref, fn_ref, o_ref, idx_s, buf, sem_i, sem_g,
                    *, final_norm):
    tc = x_ref.shape[0]
    i = pl.program_id(0)
    cp = pltpu.make_async_copy(dest_hbm.at[i], idx_s, sem_i)
    cp.start()
    cp.wait()

    def issue(tk, carry):
        for k in range(TOP_K):
            src = idx_s[k * tc + tk]
            pltpu.make_async_copy(ys_hbm.at[pl.ds(src, 1)], buf.at[k, pl.ds(tk, 1)], sem_g).start()
        return carry

    lax.fori_loop(0, tc, issue, 0, unroll=ISSUE_UNROLL)
    for k in range(TOP_K):
        pltpu.make_async_copy(ys_hbm.at[pl.ds(0, tc)], buf.at[k], sem_g).wait()

    moe = tw_ref[:, 0:1] * buf[0]
    for k in range(1, TOP_K):
        moe = moe + tw_ref[:, k:k + 1] * buf[k]
    x = x_ref[...] + gate_ref[0] * moe
    if final_norm:
        x = x * lax.rsqrt(jnp.mean(x * x, axis=-1, keepdims=True) + NORM_EPS) * fn_ref[...]
    o_ref[...] = x


def _combine(dest_tiles, ys, x2, tw_t, g2, final_w, seq, tc, final_norm):
    t, d = x2.shape
    per_b = seq // tc
    return pl.pallas_call(
        functools.partial(_combine_kernel, final_norm=final_norm),
        out_shape=jax.ShapeDtypeStruct((t, d), F32),
        grid=(t // tc,),
        in_specs=[pl.BlockSpec(memory_space=pl.ANY), pl.BlockSpec(memory_space=pl.ANY),
                  pl.BlockSpec((tc, d), lambda i: (i, 0)),
                  pl.BlockSpec((tc, TOP_K), lambda i: (i, 0)),
                  pl.BlockSpec((1, 1, d), lambda i: (i // per_b, 0, 0)),
                  pl.BlockSpec((1, d), lambda i: (0, 0))],
        out_specs=pl.BlockSpec((tc, d), lambda i: (i, 0)),
        scratch_shapes=[pltpu.SMEM((TOP_K * tc,), jnp.int32),
                        pltpu.VMEM((TOP_K, tc, d), F32),
                        pltpu.SemaphoreType.DMA, pltpu.SemaphoreType.DMA],
        compiler_params=pltpu.CompilerParams(dimension_semantics=("arbitrary",),
                                             vmem_limit_bytes=VMEM_LIMIT),
        name="moe_combine",
    )(dest_tiles, ys, x2, tw_t, g2, final_w)


def _moe(v, top_idx, top_w, rank, counts, x2, g2, final_w, p, seq, layer, final_norm):
    t, d = v.shape
    blk = 512 if t * TOP_K >= 512 * N_EXPERTS else 128
    n_blocks = t * TOP_K // blk + N_EXPERTS
    cnt = counts[:, 0].astype(jnp.int32)
    padded = (cnt + blk - 1) // blk * blk
    pend = jnp.cumsum(padded)
    pstart = pend - padded
    dest = rank + jnp.sum(jnp.where(top_idx[None] == jnp.arange(N_EXPERTS, dtype=jnp.int32)[:, None, None],
                                    pstart[:, None, None], 0), axis=0)
    block_start = jnp.arange(n_blocks, dtype=jnp.int32) * blk
    block_e = jnp.minimum(jnp.sum(block_start[:, None] >= pend[None, :], axis=1), N_EXPERTS - 1).astype(jnp.int32)
    block_e = block_e + layer * N_EXPERTS
    n_used = (pend[-1:] // blk).astype(jnp.int32)
    tg = min(512, seq)
    dest_tiles = dest.reshape(TOP_K, t // tg, tg).transpose(1, 0, 2).reshape(t // tg, TOP_K * tg)
    xs = _dispatch(dest_tiles, v, jnp.zeros((n_blocks * blk, d), F32), tg)
    ys = _experts(block_e, n_used, xs, p, blk)
    return _combine(dest_tiles, ys, x2, top_w.T, g2, final_w, seq, tg, final_norm)


def _pack_w_in(w_in):
    d = w_in.shape[0]
    sizes = (1024, 1536, 16, 1024, 1024, 1024, 1024, 1024, 1024, 3072)
    offs = np.concatenate([[0], np.cumsum(sizes)])
    z, xbc, dt, hq, hf, hi, hg, lg, lx, gates = [w_in[:, offs[k]:offs[k + 1]] for k in range(len(sizes))]
    pad = jnp.zeros((d, SEG - 512 - 16), w_in.dtype)
    return jnp.concatenate([z, xbc[:, :1024], xbc[:, 1024:], dt, pad, hq, hf, hi, hg, lg, lx, gates],
                           axis=1).astype(BF16)


def _layer_params(l, a, lower_bound):
    d = a["w_in"].shape[1]
    row = lambda v: v.reshape(1, -1).astype(F32)
    lane_pad = lambda v: jnp.pad(v.astype(F32), (0, LANE - v.shape[0])).reshape(1, LANE)
    r = np.arange(SSD_CHUNK)
    rt = min(512, a["seq"])
    return {
        "w_in": _pack_w_in(a["w_in"][l]),
        "norm_mix": row(a["norm_mix"][l]), "norm_ffn": row(a["norm_ffn"][l]),
        "cwx": a["ssd_conv_w"][l][:, :d], "cbx": row(a["ssd_conv_b"][l][:d]),
        "cwb": a["ssd_conv_w"][l][:, d:], "cbb": row(a["ssd_conv_b"][l][d:]),
        "dtb": lane_pad(a["ssd_dt_bias"][l]), "alog": lane_pad(a["ssd_a_log"][l]),
        "dskip": row(jnp.repeat(a["ssd_d"][l], SSD_HEAD_DIM)), "ssd_norm": row(a["ssd_norm"][l]),
        "tri": jnp.asarray((r[None, :] <= r[:, None]).astype(np.float32), BF16),
        "ex": jnp.asarray((np.arange(LANE)[:, None] == np.arange(d)[None, :] // SSD_HEAD_DIM)
                          .astype(np.float32), BF16),
        "lb": row(lower_bound), "hgrn_norm": row(a["hgrn_norm"][l]),
        "hgrn_tab": jnp.asarray(_hgrn_tables(), BF16),
        "lru_cw": a["lru_conv_w"][l], "lru_cb": row(a["lru_conv_b"][l]),
        "lru_wa": a["lru_wa"][l].astype(BF16), "lru_ba": row(a["lru_ba"][l]),
        "lru_wx": a["lru_wx"][l].astype(BF16), "lru_bx": row(a["lru_bx"][l]),
        "lru_lam": row(a["lru_lambda"][l]),
        "w_br_ssd": a["w_br_ssd"][l].astype(BF16), "w_br_hgrn": a["w_br_hgrn"][l].astype(BF16),
        "w_br_lru": a["w_br_lru"][l].astype(BF16), "w_out": a["w_out"][l].astype(BF16),
        "router_wt": _router_split(a["router_w"][l]), "router_b": a["router_b"][l].reshape(-1, 1).astype(F32),
        "upper": jnp.asarray((np.arange(rt)[:, None] < np.arange(rt)[None, :]).astype(np.float32), BF16),
        "w_gu": a["w_gu_all"], "b_gu": a["b_gu_all"], "w_down": a["w_down_all"], "b_down": a["b_down_all"],
    }


def _router_split(w):
    wt = w.T.astype(F32)
    hi = wt.astype(BF16)
    lo = (wt - hi.astype(F32)).astype(BF16)
    return jnp.stack([hi, lo])


def kernel(x, c, ada_w, ada_b, norm_mix, norm_ffn, w_in, ssd_conv_w, ssd_conv_b, ssd_dt_bias, ssd_a_log, ssd_d, ssd_norm, hgrn_lb, hgrn_norm, lru_conv_w, lru_conv_b, lru_wa, lru_ba, lru_wx, lru_bx, lru_lambda, w_br_ssd, w_br_hgrn, w_br_lru, w_out, router_w, router_b, moe_w_gu, moe_b_gu, moe_w_down, moe_b_down, final_norm):
    bsz, seq, d = x.shape
    depth = ada_w.shape[0]
    a = dict(seq=seq, norm_mix=norm_mix, norm_ffn=norm_ffn, w_in=w_in, ssd_conv_w=ssd_conv_w,
             ssd_conv_b=ssd_conv_b, ssd_dt_bias=ssd_dt_bias, ssd_a_log=ssd_a_log, ssd_d=ssd_d,
             ssd_norm=ssd_norm, hgrn_norm=hgrn_norm, lru_conv_w=lru_conv_w, lru_conv_b=lru_conv_b,
             lru_wa=lru_wa, lru_ba=lru_ba, lru_wx=lru_wx, lru_bx=lru_bx, lru_lambda=lru_lambda,
             w_br_ssd=w_br_ssd, w_br_hgrn=w_br_hgrn, w_br_lru=w_br_lru, w_out=w_out, router_w=router_w,
             router_b=router_b,
             w_gu_all=moe_w_gu.astype(BF16).reshape((-1,) + moe_w_gu.shape[2:]),
             b_gu_all=moe_b_gu.reshape(-1, 1, moe_b_gu.shape[-1]),
             w_down_all=moe_w_down.astype(BF16).reshape((-1,) + moe_w_down.shape[2:]),
             b_down_all=moe_b_down.reshape(-1, 1, moe_b_down.shape[-1]))
    lb_soft = jax.nn.softmax(hgrn_lb.astype(F32), axis=0)
    lower_bounds = jnp.cumsum(lb_soft, axis=0) - lb_soft[0]
    mod = _adaln(c, ada_w, ada_b).reshape(depth, bsz, 6, 1, d)
    x2 = x.reshape(bsz * seq, d)
    fw = final_norm.reshape(1, d)
    for l in range(depth):
        p = _layer_params(l, a, lower_bounds[l])
        sh1, sc1, g1, sh2, sc2, g2 = [mod[l, :, k] for k in range(6)]
        proj = _inproj(x2, sc1, sh1, p["norm_mix"], p["w_in"], seq)
        y_ssd = _ssd(proj, bsz, seq, p)
        y_hgrn = _hgrn(proj, bsz, seq, p)
        y_lru = _lru(proj, bsz, seq, p)
        x2, v, top_idx, top_w, rank, counts = _merge(y_ssd, y_hgrn, y_lru, proj, x2, g1, sc2, sh2, p, seq)
        x2 = _moe(v, top_idx, top_w, rank, counts, x2, g2, fw, p, seq, l, final_norm=(l == depth - 1))
    return x2.reshape(bsz, seq, d)
```

```python
import functools

import numpy as np
import jax
import jax.numpy as jnp
from jax import lax
from jax.experimental import pallas as pl
from jax.experimental.pallas import tpu as pltpu

F32 = jnp.float32
BF16 = jnp.bfloat16

NORM_EPS = 1e-6
CONV_WIDTH = 4
CHUNK = 64
SSD_CHUNK = 128
SSD_HEADS = 16
SSD_HEAD_DIM = 64
SSD_GROUPS = 2
SSD_STATE = 128
HGRN_HEADS = 8
HGRN_HEAD_DIM = 128
GATE_LOGIT_CLIP = 30.0
LRU_BLOCKS = 4
LRU_C = 8.0
LRU_SCAN_SUB = 32
N_EXPERTS = 32
TOP_K = 4
SWIGLU_LIMIT = 7.0
SWIGLU_ALPHA = 1.702
HGRN_LEVELS = (64, 32, 16, 8, 4, 2)
HGRN_VPU_LEVELS = 3

LANE = 128
SUBLANE = 8
VMEM_LIMIT = 48 * 1024 * 1024
ISSUE_UNROLL = 8
DISPATCH_TILE = 2048
COMBINE_TILE = 1024

SEG = 1024
COL_Z, COL_XS, COL_BCDT, COL_HQ, COL_HF, COL_HI, COL_HG, COL_LG, COL_LX, COL_G0 = range(10)
N_PROJ = 12 * SEG


def _sigmoid(x):
    return jax.nn.sigmoid(x)


def _silu(x):
    return x * _sigmoid(x)


def _softplus(x):
    return jnp.maximum(x, 0.0) + jnp.log1p(jnp.exp(-jnp.abs(x)))


def _log_sigmoid(x):
    return jnp.minimum(x, 0.0) - jnp.log1p(jnp.exp(-jnp.abs(x)))


def _neg_expm1(x):
    t = jnp.tanh(0.5 * x)
    return -2.0 * t / (1.0 - t)


def _gelu_tanh(x):
    return 0.5 * x * (1.0 + jnp.tanh(0.7978845608028654 * (x + 0.044715 * (x * x * x))))


def _split3(x):
    hi = x.astype(BF16)
    r1 = x - hi.astype(F32)
    mid = r1.astype(BF16)
    lo = (r1 - mid.astype(F32)).astype(BF16)
    return hi, mid, lo


def _exact_left_dot(p_bf16, x):
    hi, mid, lo = _split3(x)
    d = lambda a: jnp.dot(p_bf16, a, preferred_element_type=F32)
    return d(hi) + d(mid) + d(lo)


def _dot_nt(a, b):
    return lax.dot_general(a, b, (((1,), (1,)), ((), ())), preferred_element_type=F32)


def _dot_tn(a, b):
    return lax.dot_general(a, b, (((0,), (0,)), ((), ())), preferred_element_type=F32)


def _causal_conv(raw, carry_ref, w_ref, b_ref):
    n = raw.shape[0]
    ext = jnp.concatenate([carry_ref[...], raw], axis=0)
    y = b_ref[...] + w_ref[CONV_WIDTH - 1:CONV_WIDTH, :] * raw
    for k in range(CONV_WIDTH - 1):
        shifted = pltpu.roll(ext, CONV_WIDTH - 1 - k, 0)[SUBLANE:SUBLANE + n]
        y = y + w_ref[k:k + 1, :] * shifted
    carry_ref[...] = raw[n - SUBLANE:n]
    return y


def _adaln_kernel(c_ref, w_ref, b_ref, o_ref):
    c = c_ref[...]
    o_ref[0] = jnp.dot(_silu(c), w_ref[0], precision=lax.Precision.HIGHEST,
                       preferred_element_type=F32) + b_ref[0]


def _adaln(c, ada_w, ada_b):
    depth, d, n = ada_w.shape
    bsz = c.shape[0]
    tn = 1024
    return pl.pallas_call(
        _adaln_kernel,
        out_shape=jax.ShapeDtypeStruct((depth, bsz, n), F32),
        grid=(depth, n // tn),
        in_specs=[pl.BlockSpec((bsz, d), lambda l, j: (0, 0)),
                  pl.BlockSpec((1, d, tn), lambda l, j: (l, 0, j)),
                  pl.BlockSpec((1, 1, tn), lambda l, j: (l, 0, j))],
        out_specs=pl.BlockSpec((1, bsz, tn), lambda l, j: (l, 0, j)),
        compiler_params=pltpu.CompilerParams(dimension_semantics=("arbitrary", "arbitrary")),
        name="adaln",
    )(c, ada_w, ada_b.reshape(depth, 1, n))


def _inproj_kernel(x_ref, sc_ref, sh_ref, g_ref, w_ref, o_ref, u_ref):
    @pl.when(pl.program_id(1) == 0)
    def _():
        x = x_ref[...]
        y = x * lax.rsqrt(jnp.mean(x * x, axis=-1, keepdims=True) + NORM_EPS) * g_ref[...]
        u_ref[...] = (y * (1.0 + sc_ref[0]) + sh_ref[0]).astype(BF16)

    o_ref[...] = jnp.dot(u_ref[...], w_ref[...], preferred_element_type=F32)


def _inproj(x2, sc, sh, g, w, seq):
    t, d = x2.shape
    n = w.shape[1]
    tm = min(1024, seq)
    tn = 2048
    per_b = seq // tm
    return pl.pallas_call(
        _inproj_kernel,
        out_shape=jax.ShapeDtypeStruct((t, n), F32),
        grid=(t // tm, n // tn),
        in_specs=[pl.BlockSpec((tm, d), lambda i, j: (i, 0)),
                  pl.BlockSpec((1, 1, d), lambda i, j: (i // per_b, 0, 0)),
                  pl.BlockSpec((1, 1, d), lambda i, j: (i // per_b, 0, 0)),
                  pl.BlockSpec((1, d), lambda i, j: (0, 0)),
                  pl.BlockSpec((d, tn), lambda i, j: (0, j))],
        out_specs=pl.BlockSpec((tm, tn), lambda i, j: (i, j)),
        scratch_shapes=[pltpu.VMEM((tm, d), BF16)],
        compiler_params=pltpu.CompilerParams(dimension_semantics=("arbitrary", "arbitrary"),
                                             vmem_limit_bytes=VMEM_LIMIT),
        name="inproj",
    )(x2, sc, sh, g, w)


def _ssd_kernel(z_ref, xs_ref, bcdt_ref, cwx_ref, cbx_ref, cwb_ref, cbb_ref, dtb_ref, alog_ref,
                dskip_ref, nw_ref, tri_ref, ex_ref, o_ref, cx_ref, cb_ref, st_ref, xs_s, bc_s, dt_s, y_s):
    ts = z_ref.shape[0]
    p, nst, hpg = SSD_HEAD_DIM, SSD_STATE, SSD_HEADS // SSD_GROUPS

    @pl.when(pl.program_id(1) == 0)
    def _():
        cx_ref[...] = jnp.zeros_like(cx_ref)
        cb_ref[...] = jnp.zeros_like(cb_ref)
        st_ref[...] = jnp.zeros_like(st_ref)

    xs_s[...] = _silu(_causal_conv(xs_ref[...], cx_ref, cwx_ref, cbx_ref))
    bc_s[...] = _silu(_causal_conv(bcdt_ref[:, 0:4 * nst], cb_ref, cwb_ref, cbb_ref))
    dt_s[...] = _softplus(bcdt_ref[:, 4 * nst:4 * nst + LANE] + dtb_ref[...])
    a_neg = -jnp.exp(alog_ref[...])
    lc = SSD_CHUNK
    li = lax.broadcasted_iota(jnp.int32, (lc, lc), 0)
    si = lax.broadcasted_iota(jnp.int32, (lc, lc), 1)
    causal = li >= si
    first_half = lax.broadcasted_iota(jnp.int32, (lc, 2 * p), 1) < p

    def chunk_body(c, carry):
        r0 = pl.multiple_of(c * lc, lc)
        rows = pl.ds(r0, lc)
        dt = dt_s[rows, :]
        acs = _exact_left_dot(tri_ref[...], dt * a_neg)
        acs_t = acs.T
        pieces = jnp.concatenate(list(_split3(acs)) + list(_split3(dt)), axis=0)
        spread = jnp.dot(pieces, ex_ref[...], preferred_element_type=F32)
        acs_x = spread[0:lc] + spread[lc:2 * lc] + spread[2 * lc:3 * lc]
        dt_x = spread[3 * lc:4 * lc] + spread[4 * lc:5 * lc] + spread[5 * lc:6 * lc]
        xs_c = xs_s[rows, :]
        xdt = xs_c * dt_x
        xdte = (xdt * jnp.exp(acs_x[lc - 1:lc, :] - acs_x)).astype(BF16)
        grow = jnp.exp(acs_x)
        for g in range(SSD_GROUPS):
            b_g = bc_s[rows, g * nst:(g + 1) * nst].astype(BF16)
            c_g = bc_s[rows, (SSD_GROUPS + g) * nst:(SSD_GROUPS + g + 1) * nst].astype(BF16)
            cbm = _dot_nt(c_g, b_g)
            prev = st_ref[g * hpg:(g + 1) * hpg].reshape(hpg * p, nst)
            y_off = _dot_nt(c_g, prev.astype(BF16))
            for pr in range(hpg // 2):
                ha = g * hpg + 2 * pr
                pair = slice(ha * p, (ha + 2) * p)
                xp = xdt[:, pair]
                yd = None
                for k in range(2):
                    h = ha + k
                    col = acs[:, h:h + 1]
                    row = acs_t[h:h + 1, :]
                    decay = jnp.where(causal, jnp.exp(jnp.where(causal, col - row, 0.0)), 0.0)
                    x_k = jnp.where(first_half if k == 0 else ~first_half, xp, 0.0).astype(BF16)
                    part = jnp.dot((cbm * decay).astype(BF16), x_k, preferred_element_type=F32)
                    yd = part if yd is None else yd + part
                lo = 2 * pr * p
                y_s[rows, pair] = (yd + grow[:, pair] * y_off[:, lo:lo + 2 * p]
                                   + xs_c[:, pair] * dskip_ref[:, pair])
                upd = _dot_tn(xdte[:, pair], b_g)
                for k in range(2):
                    h = ha + k
                    st_ref[h] = st_ref[h] * jnp.exp(acs[lc - 1:lc, h:h + 1]) + upd[k * p:(k + 1) * p]
        return carry

    lax.fori_loop(0, ts // lc, chunk_body, 0)

    y = y_s[...] * _silu(z_ref[...])
    half = y.shape[1] // SSD_GROUPS
    outs = []
    for g in range(SSD_GROUPS):
        yg = y[:, g * half:(g + 1) * half]
        outs.append(yg * lax.rsqrt(jnp.mean(yg * yg, axis=-1, keepdims=True) + NORM_EPS))
    o_ref[...] = (jnp.concatenate(outs, axis=1) * nw_ref[...]).astype(o_ref.dtype)


def _ssd(proj, bsz, seq, p):
    t = proj.shape[0]
    ts = min(256, seq)
    per_b = seq // ts
    d = SEG
    row = lambda col: pl.BlockSpec((ts, d), lambda b, s: (b * per_b + s, col))
    const = lambda shape: pl.BlockSpec(shape, lambda b, s: (0,) * len(shape))
    return pl.pallas_call(
        _ssd_kernel,
        out_shape=jax.ShapeDtypeStruct((t, d), BF16),
        grid=(bsz, per_b),
        in_specs=[row(COL_Z), row(COL_XS), row(COL_BCDT),
                  const((CONV_WIDTH, d)), const((1, d)), const((CONV_WIDTH, 4 * SSD_STATE)),
                  const((1, 4 * SSD_STATE)), const((1, LANE)), const((1, LANE)), const((1, d)),
                  const((1, d)), const((SSD_CHUNK, SSD_CHUNK)), const((LANE, d))],
        out_specs=pl.BlockSpec((ts, d), lambda b, s: (b * per_b + s, 0)),
        scratch_shapes=[pltpu.VMEM((SUBLANE, d), F32), pltpu.VMEM((SUBLANE, 4 * SSD_STATE), F32),
                        pltpu.VMEM((SSD_HEADS, SSD_HEAD_DIM, SSD_STATE), F32),
                        pltpu.VMEM((ts, d), F32), pltpu.VMEM((ts, 4 * SSD_STATE), F32),
                        pltpu.VMEM((ts, LANE), F32), pltpu.VMEM((ts, d), F32)],
        compiler_params=pltpu.CompilerParams(dimension_semantics=("arbitrary", "arbitrary"),
                                             vmem_limit_bytes=VMEM_LIMIT),
        name="ssd",
    )(proj, proj, proj, p["cwx"], p["cbx"], p["cwb"], p["cbb"], p["dtb"], p["alog"], p["dskip"],
      p["ssd_norm"], p["tri"], p["ex"])


def _hgrn_tables():
    n = CHUNK
    r = np.arange(n)[:, None]
    j = np.arange(n)[None, :]
    mats = [j <= r]
    for w in HGRN_LEVELS[HGRN_VPU_LEVELS:]:
        mid = r - r % w + w // 2
        mats.append(np.where(r >= mid, (j >= mid) & (j <= r), (j > r) & (j < mid)))
    return np.concatenate(mats, axis=0).astype(np.float32)


def _hgrn_kernel(q_ref, f_ref, i_ref, g_ref, lb_ref, nw_ref, pm_ref, o_ref, st_ref, o_s):
    ts = q_ref.shape[0]
    dk = HGRN_HEAD_DIM

    @pl.when(pl.program_id(1) == 0)
    def _():
        st_ref[...] = jnp.zeros_like(st_ref)

    ti = lax.broadcasted_iota(jnp.int32, (CHUNK, CHUNK), 0)
    si = lax.broadcasted_iota(jnp.int32, (CHUNK, CHUNK), 1)
    txs = ti ^ si
    ri = lax.broadcasted_iota(jnp.int32, (CHUNK, 1), 0)

    def chunk_body(c, carry):
        r0 = pl.multiple_of(c * CHUNK, CHUNK)
        rows = pl.ds(r0, CHUNK)
        ff = f_ref[rows, :]
        log_f = _log_sigmoid(ff) + jnp.log1p(lb_ref[...] * jnp.exp(-jnp.maximum(ff, -GATE_LOGIT_CLIP)))
        log_f = jnp.minimum(log_f, 0.0)
        kf = 1.0 - jnp.exp(log_f)
        qf = _silu(q_ref[rows, :])
        vv = i_ref[rows, :]
        gsum = _exact_left_dot(pm_ref[...], log_f)
        for h in range(HGRN_HEADS):
            hs = slice(h * dk, (h + 1) * dk)
            qh, kh = qf[:, hs], kf[:, hs]
            vh = vv[:, hs].astype(BF16)
            bc = gsum[0:CHUNK, hs]
            b_end = bc[CHUNK - 1:CHUNK, :]
            attn = jnp.where(txs == 0, _dot_nt(qh.astype(BF16), kh.astype(BF16)), 0.0)
            for lv, w in enumerate(HGRN_LEVELS):
                if lv < HGRN_VPU_LEVELS:
                    mid = jnp.concatenate([jnp.broadcast_to(bc[m + w // 2 - 1:m + w // 2, :], (w, dk))
                                           for m in range(0, CHUNK, w)], axis=0)
                    e = jnp.exp(-jnp.abs(bc - mid))
                else:
                    t0 = (lv - HGRN_VPU_LEVELS + 1) * CHUNK
                    e = jnp.exp(gsum[t0:t0 + CHUNK, hs])
                upper = (ri & (w // 2)) != 0
                qt = jnp.where(upper, qh * e, 0.0).astype(BF16)
                kt = jnp.where(upper, 0.0, kh * e).astype(BF16)
                attn = attn + jnp.where(txs < w, _dot_nt(qt, kt), 0.0)
            o = jnp.dot(attn.astype(BF16), vh, preferred_element_type=F32)
            state = st_ref[h]
            qs = (qh * jnp.exp(bc)).astype(BF16)
            o = o + _dot_nt(qs, state.astype(BF16))
            kd = (kh * jnp.exp(b_end - bc)).astype(BF16)
            st_ref[h] = state * jnp.exp(b_end) + _dot_tn(vh, kd)
            o = o * lax.rsqrt(jnp.mean(o * o, axis=-1, keepdims=True) + NORM_EPS)
            o_s[rows, hs] = o
        return carry

    lax.fori_loop(0, ts // CHUNK, chunk_body, 0, unroll=2)
    o_ref[...] = (o_s[...] * nw_ref[...] * _silu(g_ref[...])).astype(o_ref.dtype)


def _hgrn(proj, bsz, seq, p):
    t = proj.shape[0]
    ts = min(256, seq)
    per_b = seq // ts
    d = SEG
    row = lambda col: pl.BlockSpec((ts, d), lambda b, s: (b * per_b + s, col))
    const = lambda shape: pl.BlockSpec(shape, lambda b, s: (0,) * len(shape))
    n_tab = (1 + len(HGRN_LEVELS) - HGRN_VPU_LEVELS) * CHUNK
    return pl.pallas_call(
        _hgrn_kernel,
        out_shape=jax.ShapeDtypeStruct((t, d), BF16),
        grid=(bsz, per_b),
        in_specs=[row(COL_HQ), row(COL_HF), row(COL_HI), row(COL_HG),
                  const((1, d)), const((1, d)), const((n_tab, CHUNK))],
        out_specs=pl.BlockSpec((ts, d), lambda b, s: (b * per_b + s, 0)),
        scratch_shapes=[pltpu.VMEM((HGRN_HEADS, HGRN_HEAD_DIM, HGRN_HEAD_DIM), F32),
                        pltpu.VMEM((ts, d), F32)],
        compiler_params=pltpu.CompilerParams(dimension_semantics=("arbitrary", "arbitrary"),
                                             vmem_limit_bytes=VMEM_LIMIT),
        name="hgrn2",
    )(proj, proj, proj, proj, p["lb"], p["hgrn_norm"], p["hgrn_tab"])


def _lru_kernel(gate_ref, x_ref, cw_ref, cb_ref, wa_ref, ba_ref, wx_ref, bx_ref, lam_ref,
                o_ref, cx_ref, h_ref):
    ts, d = x_ref.shape
    bd = d // LRU_BLOCKS
    first = pl.program_id(1) == 0

    @pl.when(first)
    def _():
        cx_ref[...] = jnp.zeros_like(cx_ref)
        h_ref[...] = jnp.zeros_like(h_ref)

    xc = _causal_conv(x_ref[...], cx_ref, cw_ref, cb_ref)
    xcb = xc.astype(BF16)
    ra, rx = [], []
    for i in range(LRU_BLOCKS):
        blk = xcb[:, i * bd:(i + 1) * bd]
        ra.append(jnp.dot(blk, wa_ref[i], preferred_element_type=F32))
        rx.append(jnp.dot(blk, wx_ref[i], preferred_element_type=F32))
    r = _sigmoid(jnp.concatenate(ra, axis=1) + ba_ref[...])
    ig = _sigmoid(jnp.concatenate(rx, axis=1) + bx_ref[...])
    log_a = -LRU_C * r * _softplus(-lam_ref[...])
    a = jnp.exp(log_a)
    mult = jnp.sqrt(jnp.maximum(_neg_expm1(2.0 * log_a), 0.0))
    rowi = lax.broadcasted_iota(jnp.int32, (ts, 1), 0)
    mult = jnp.where(jnp.logical_and(first, rowi == 0), 1.0, mult)
    u = mult * ig * xc
    sub = min(LRU_SCAN_SUB, ts)
    rmod = rowi & (sub - 1)
    sh = 1
    while sh < sub:
        keep = rmod >= sh
        a_prev = jnp.where(keep, pltpu.roll(a, sh, 0), 1.0)
        u_prev = jnp.where(keep, pltpu.roll(u, sh, 0), 0.0)
        u = a * u_prev + u
        a = a * a_prev
        sh *= 2
    carry = h_ref[0:1, :]
    parts = []
    for j in range(ts // sub):
        hj = u[j * sub:(j + 1) * sub] + a[j * sub:(j + 1) * sub] * carry
        carry = hj[sub - 1:sub]
        parts.append(hj)
    h = jnp.concatenate(parts, axis=0)
    h_ref[...] = jnp.broadcast_to(h[ts - 1:ts, :], h_ref.shape)
    o_ref[...] = (h * _gelu_tanh(gate_ref[...])).astype(o_ref.dtype)


def _lru(proj, bsz, seq, p):
    t = proj.shape[0]
    ts = min(256, seq)
    per_b = seq // ts
    d = SEG
    bd = d // LRU_BLOCKS
    row = lambda col: pl.BlockSpec((ts, d), lambda b, s: (b * per_b + s, col))
    const = lambda shape: pl.BlockSpec(shape, lambda b, s: (0,) * len(shape))
    return pl.pallas_call(
        _lru_kernel,
        out_shape=jax.ShapeDtypeStruct((t, d), BF16),
        grid=(bsz, per_b),
        in_specs=[row(COL_LG), row(COL_LX), const((CONV_WIDTH, d)), const((1, d)),
                  const((LRU_BLOCKS, bd, bd)), const((1, d)), const((LRU_BLOCKS, bd, bd)), const((1, d)),
                  const((1, d))],
        out_specs=pl.BlockSpec((ts, d), lambda b, s: (b * per_b + s, 0)),
        scratch_shapes=[pltpu.VMEM((SUBLANE, d), F32), pltpu.VMEM((SUBLANE, d), F32)],
        compiler_params=pltpu.CompilerParams(dimension_semantics=("arbitrary", "arbitrary"),
                                             vmem_limit_bytes=VMEM_LIMIT),
        name="rglru",
    )(proj, proj, p["lru_cw"], p["lru_cb"], p["lru_wa"], p["lru_ba"], p["lru_wx"], p["lru_bx"],
      p["lru_lam"])


def _merge_kernel(ys_ref, yh_ref, yl_ref, g0_ref, g1_ref, g2_ref, x_ref, gate_ref, sc_ref, sh_ref,
                  nf_ref, wbs_ref, wbh_ref, wbl_ref, wo_ref, rwt_ref, rb_ref, up_ref,
                  xo_ref, v_ref, idx_ref, tw_ref, rank_ref, cnt_ref, carry_ref):
    tm = x_ref.shape[0]

    @pl.when(pl.program_id(0) == 0)
    def _():
        carry_ref[...] = jnp.zeros_like(carry_ref)

    def branch(y_ref, g_ref, w_ref):
        return _sigmoid(g_ref[...]) * jnp.dot(y_ref[...], w_ref[...], preferred_element_type=F32)

    merged = branch(ys_ref, g0_ref, wbs_ref) + branch(yh_ref, g1_ref, wbh_ref) + branch(yl_ref, g2_ref, wbl_ref)
    mix = jnp.dot(merged.astype(BF16), wo_ref[...], preferred_element_type=F32)
    x = x_ref[...] + gate_ref[0] * mix
    xo_ref[...] = x
    v = x * lax.rsqrt(jnp.mean(x * x, axis=-1, keepdims=True) + NORM_EPS) * nf_ref[...]
    v = v * (1.0 + sc_ref[0]) + sh_ref[0]
    v_ref[...] = v

    v_hi = v.astype(BF16)
    v_lo = (v - v_hi.astype(F32)).astype(BF16)
    logits = (_dot_nt(rwt_ref[0], v_hi) + _dot_nt(rwt_ref[0], v_lo) + _dot_nt(rwt_ref[1], v_hi)
              + rb_ref[...])
    ei = lax.broadcasted_iota(jnp.int32, (N_EXPERTS, tm), 0)
    work = logits
    tops, sel = [], []
    for k in range(TOP_K):
        m = jnp.max(work, axis=0, keepdims=True)
        idx = jnp.min(jnp.where(work == m, ei, N_EXPERTS), axis=0, keepdims=True)
        hit = ei == idx
        tops.append(m)
        sel.append(hit)
        idx_ref[k:k + 1, :] = idx
        work = jnp.where(hit, -jnp.inf, work)
    exps = [jnp.exp(m - tops[0]) for m in tops]
    denom = exps[0] + exps[1] + exps[2] + exps[3]
    for k in range(TOP_K):
        tw_ref[k:k + 1, :] = exps[k] / denom

    member = jnp.where(sel[0] | sel[1] | sel[2] | sel[3], 1.0, 0.0)
    before = jnp.dot(member.astype(BF16), up_ref[...], preferred_element_type=F32) + carry_ref[:, 0:1]
    for k in range(TOP_K):
        rank_ref[k:k + 1, :] = jnp.sum(jnp.where(sel[k], before, 0.0), axis=0, keepdims=True).astype(jnp.int32)
    carry_ref[...] = carry_ref[...] + jnp.sum(member, axis=1, keepdims=True)
    cnt_ref[...] = carry_ref[...]


def _merge(y_ssd, y_hgrn, y_lru, proj, x2, g1, sc2, sh2, p, seq):
    t, d = x2.shape
    tm = min(512, seq)
    per_b = seq // tm
    row = lambda col: pl.BlockSpec((tm, d), lambda i: (i, col))
    mod = pl.BlockSpec((1, 1, d), lambda i: (i // per_b, 0, 0))
    const = lambda shape: pl.BlockSpec(shape, lambda i: (0,) * len(shape))
    tok = lambda: pl.BlockSpec((TOP_K, tm), lambda i: (0, i))
    return pl.pallas_call(
        _merge_kernel,
        out_shape=(jax.ShapeDtypeStruct((t, d), F32), jax.ShapeDtypeStruct((t, d), F32),
                   jax.ShapeDtypeStruct((TOP_K, t), jnp.int32), jax.ShapeDtypeStruct((TOP_K, t), F32),
                   jax.ShapeDtypeStruct((TOP_K, t), jnp.int32), jax.ShapeDtypeStruct((N_EXPERTS, LANE), F32)),
        grid=(t // tm,),
        in_specs=[row(0), row(0), row(0), row(COL_G0), row(COL_G0 + 1), row(COL_G0 + 2), row(0),
                  mod, mod, mod, const((1, d)), const((d, d)), const((d, d)), const((d, d)), const((d, d)),
                  const((2, N_EXPERTS, d)), const((N_EXPERTS, 1)), const((tm, tm))],
        out_specs=(row(0), row(0), tok(), tok(), tok(), const((N_EXPERTS, LANE))),
        scratch_shapes=[pltpu.VMEM((N_EXPERTS, LANE), F32)],
        compiler_params=pltpu.CompilerParams(dimension_semantics=("arbitrary",),
                                             vmem_limit_bytes=VMEM_LIMIT),
        name="merge_router",
    )(y_ssd, y_hgrn, y_lru, proj, proj, proj, x2, g1, sc2, sh2, p["norm_ffn"], p["w_br_ssd"],
      p["w_br_hgrn"], p["w_br_lru"], p["w_out"], p["router_wt"], p["router_b"], p["upper"])


def _dispatch_kernel(dest_hbm, v_ref, xs_in, xs_hbm, idx_s, sem_i, sem_o):
    del xs_in
    tg = v_ref.shape[0]
    i = pl.program_id(0)
    cp = pltpu.make_async_copy(dest_hbm.at[i], idx_s, sem_i)
    cp.start()
    cp.wait()

    def issue(tk, carry):
        for k in range(TOP_K):
            dst = idx_s[k * tg + tk]
            pltpu.make_async_copy(v_ref.at[pl.ds(tk, 1)], xs_hbm.at[pl.ds(dst, 1)], sem_o).start()
        return carry

    lax.fori_loop(0, tg, issue, 0, unroll=ISSUE_UNROLL)
    for k in range(TOP_K):
        pltpu.make_async_copy(v_ref, xs_hbm.at[pl.ds(0, tg)], sem_o).wait()


def _dispatch(dest_tiles, v, xs_zero, tg):
    t, d = v.shape
    n_rows = xs_zero.shape[0]
    return pl.pallas_call(
        _dispatch_kernel,
        out_shape=jax.ShapeDtypeStruct((n_rows, d), F32),
        grid=(t // tg,),
        in_specs=[pl.BlockSpec(memory_space=pl.ANY),
                  pl.BlockSpec((tg, d), lambda i: (i, 0)),
                  pl.BlockSpec(memory_space=pl.ANY)],
        out_specs=pl.BlockSpec(memory_space=pl.ANY),
        scratch_shapes=[pltpu.SMEM((TOP_K * tg,), jnp.int32),
                        pltpu.SemaphoreType.DMA, pltpu.SemaphoreType.DMA],
        input_output_aliases={2: 0},
        compiler_params=pltpu.CompilerParams(dimension_semantics=("arbitrary",)),
        name="moe_dispatch",
    )(dest_tiles, v, xs_zero)


def _expert_kernel(be_ref, nu_ref, x_ref, wgu_ref, bgu_ref, wd_ref, bd_ref, o_ref):
    del be_ref
    i = pl.program_id(0)
    de = wd_ref.shape[1]

    @pl.when(i < nu_ref[0])
    def _():
        gu = jnp.dot(x_ref[...].astype(BF16), wgu_ref[0], preferred_element_type=F32) + bgu_ref[0]
        glu = jnp.minimum(gu[:, :de], SWIGLU_LIMIT)
        lin = jnp.clip(gu[:, de:], -SWIGLU_LIMIT, SWIGLU_LIMIT)
        act = glu * _sigmoid(SWIGLU_ALPHA * glu) * (lin + 1.0)
        o_ref[...] = jnp.dot(act.astype(BF16), wd_ref[0], preferred_element_type=F32) + bd_ref[0]

    @pl.when(i >= nu_ref[0])
    def _():
        o_ref[...] = jnp.zeros_like(o_ref)


def _experts(block_e, n_used, xs, p, blk):
    n_rows, d = xs.shape
    n_blocks = n_rows // blk
    de = p["w_down"].shape[1]
    return pl.pallas_call(
        _expert_kernel,
        out_shape=jax.ShapeDtypeStruct((n_rows, d), F32),
        grid_spec=pltpu.PrefetchScalarGridSpec(
            num_scalar_prefetch=2,
            grid=(n_blocks,),
            in_specs=[pl.BlockSpec((blk, d), lambda i, be, nu: (i, 0)),
                      pl.BlockSpec((1, d, 2 * de), lambda i, be, nu: (be[i], 0, 0)),
                      pl.BlockSpec((1, 1, 2 * de), lambda i, be, nu: (be[i], 0, 0)),
                      pl.BlockSpec((1, de, d), lambda i, be, nu: (be[i], 0, 0)),
                      pl.BlockSpec((1, 1, d), lambda i, be, nu: (be[i], 0, 0))],
            out_specs=pl.BlockSpec((blk, d), lambda i, be, nu: (i, 0))),
        compiler_params=pltpu.CompilerParams(dimension_semantics=("arbitrary",),
                                             vmem_limit_bytes=VMEM_LIMIT),
        name="moe_experts",
    )(block_e, n_used, xs, p["w_gu"], p["b_gu"], p["w_down"], p["b_down"])


def _combine_kernel(dest_hbm, ys_hbm, x_ref, tw_ref, gate_ref, fn_ref, o_ref, idx_s, buf, sem_i, sem_g,
                    *, final_norm):
    tc = x_ref.shape[0]
    i = pl.program_id(0)
    cp = pltpu.make_async_copy(dest_hbm.at[i], idx_s, sem_i)
    cp.start()
    cp.wait()

    def issue(tk, carry):
        for k in range(TOP_K):
            src = idx_s[k * tc + tk]
            pltpu.make_async_copy(ys_hbm.at[pl.ds(src, 1)], buf.at[k, pl.ds(tk, 1)], sem_g).start()
        return carry

    lax.fori_loop(0, tc, issue, 0, unroll=ISSUE_UNROLL)
    for k in range(TOP_K):
        pltpu.make_async_copy(ys_hbm.at[pl.ds(0, tc)], buf.at[k], sem_g).wait()

    moe = tw_ref[:, 0:1] * buf[0]
    for k in range(1, TOP_K):
        moe = moe + tw_ref[:, k:k + 1] * buf[k]
    x = x_ref[...] + gate_ref[0] * moe
    if final_norm:
        x = x * lax.rsqrt(jnp.mean(x * x, axis=-1, keepdims=True) + NORM_EPS) * fn_ref[...]
    o_ref[...] = x


def _combine(dest_tiles, ys, x2, tw_t, g2, final_w, seq, tc, final_norm):
    t, d = x2.shape
    per_b = seq // tc
    return pl.pallas_call(
        functools.partial(_combine_kernel, final_norm=final_norm),
        out_shape=jax.ShapeDtypeStruct((t, d), F32),
        grid=(t // tc,),
        in_specs=[pl.BlockSpec(memory_space=pl.ANY), pl.BlockSpec(memory_space=pl.ANY),
                  pl.BlockSpec((tc, d), lambda i: (i, 0)),
                  pl.BlockSpec((tc, TOP_K), lambda i: (i, 0)),
                  pl.BlockSpec((1, 1, d), lambda i: (i // per_b, 0, 0)),
                  pl.BlockSpec((1, d), lambda i: (0, 0))],
        out_specs=pl.BlockSpec((tc, d), lambda i: (i, 0)),
        scratch_shapes=[pltpu.SMEM((TOP_K * tc,), jnp.int32),
                        pltpu.VMEM((TOP_K, tc, d), F32),
                        pltpu.SemaphoreType.DMA, pltpu.SemaphoreType.DMA],
        compiler_params=pltpu.CompilerParams(dimension_semantics=("arbitrary",),
                                             vmem_limit_bytes=VMEM_LIMIT),
        name="moe_combine",
    )(dest_tiles, ys, x2, tw_t, g2, final_w)


def _moe(v, top_idx, top_w, rank, counts, x2, g2, final_w, p, seq, layer, final_norm):
    t, d = v.shape
    blk = 512 if t * TOP_K >= 512 * N_EXPERTS else 128
    n_blocks = t * TOP_K // blk + N_EXPERTS
    cnt = counts[:, 0].astype(jnp.int32)
    padded = (cnt + blk - 1) // blk * blk
    pend = jnp.cumsum(padded)
    pstart = pend - padded
    dest = rank + jnp.sum(jnp.where(top_idx[None] == jnp.arange(N_EXPERTS, dtype=jnp.int32)[:, None, None],
                                    pstart[:, None, None], 0), axis=0)
    block_start = jnp.arange(n_blocks, dtype=jnp.int32) * blk
    block_e = jnp.minimum(jnp.sum(block_start[:, None] >= pend[None, :], axis=1), N_EXPERTS - 1).astype(jnp.int32)
    block_e = block_e + layer * N_EXPERTS
    n_used = (pend[-1:] // blk).astype(jnp.int32)
    tiles = lambda n: dest.reshape(TOP_K, t // n, n).transpose(1, 0, 2).reshape(t // n, TOP_K * n)
    tg, tc = min(DISPATCH_TILE, seq), min(COMBINE_TILE, seq)
    xs = _dispatch(tiles(tg), v, jnp.zeros((n_blocks * blk, d), F32), tg)
    ys = _experts(block_e, n_used, xs, p, blk)
    return _combine(tiles(tc), ys, x2, top_w.T, g2, final_w, seq, tc, final_norm)


def _pack_w_in(w_in):
    d = w_in.shape[0]
    sizes = (1024, 1536, 16, 1024, 1024, 1024, 1024, 1024, 1024, 3072)
    offs = np.concatenate([[0], np.cumsum(sizes)])
    z, xbc, dt, hq, hf, hi, hg, lg, lx, gates = [w_in[:, offs[k]:offs[k + 1]] for k in range(len(sizes))]
    pad = jnp.zeros((d, SEG - 512 - 16), w_in.dtype)
    return jnp.concatenate([z, xbc[:, :1024], xbc[:, 1024:], dt, pad, hq, hf, hi, hg, lg, lx, gates],
                           axis=1).astype(BF16)


def _layer_params(l, a, lower_bound):
    d = a["w_in"].shape[1]
    row = lambda v: v.reshape(1, -1).astype(F32)
    lane_pad = lambda v: jnp.pad(v.astype(F32), (0, LANE - v.shape[0])).reshape(1, LANE)
    r = np.arange(SSD_CHUNK)
    rt = min(512, a["seq"])
    return {
        "w_in": _pack_w_in(a["w_in"][l]),
        "norm_mix": row(a["norm_mix"][l]), "norm_ffn": row(a["norm_ffn"][l]),
        "cwx": a["ssd_conv_w"][l][:, :d], "cbx": row(a["ssd_conv_b"][l][:d]),
        "cwb": a["ssd_conv_w"][l][:, d:], "cbb": row(a["ssd_conv_b"][l][d:]),
        "dtb": lane_pad(a["ssd_dt_bias"][l]), "alog": lane_pad(a["ssd_a_log"][l]),
        "dskip": row(jnp.repeat(a["ssd_d"][l], SSD_HEAD_DIM)), "ssd_norm": row(a["ssd_norm"][l]),
        "tri": jnp.asarray((r[None, :] <= r[:, None]).astype(np.float32), BF16),
        "ex": jnp.asarray((np.arange(LANE)[:, None] == np.arange(d)[None, :] // SSD_HEAD_DIM)
                          .astype(np.float32), BF16),
        "lb": row(lower_bound), "hgrn_norm": row(a["hgrn_norm"][l]),
        "hgrn_tab": jnp.asarray(_hgrn_tables(), BF16),
        "lru_cw": a["lru_conv_w"][l], "lru_cb": row(a["lru_conv_b"][l]),
        "lru_wa": a["lru_wa"][l].astype(BF16), "lru_ba": row(a["lru_ba"][l]),
        "lru_wx": a["lru_wx"][l].astype(BF16), "lru_bx": row(a["lru_bx"][l]),
        "lru_lam": row(a["lru_lambda"][l]),
        "w_br_ssd": a["w_br_ssd"][l].astype(BF16), "w_br_hgrn": a["w_br_hgrn"][l].astype(BF16),
        "w_br_lru": a["w_br_lru"][l].astype(BF16), "w_out": a["w_out"][l].astype(BF16),
        "router_wt": _router_split(a["router_w"][l]), "router_b": a["router_b"][l].reshape(-1, 1).astype(F32),
        "upper": jnp.asarray((np.arange(rt)[:, None] < np.arange(rt)[None, :]).astype(np.float32), BF16),
        "w_gu": a["w_gu_all"], "b_gu": a["b_gu_all"], "w_down": a["w_down_all"], "b_down": a["b_down_all"],
    }


def _router_split(w):
    wt = w.T.astype(F32)
    hi = wt.astype(BF16)
    lo = (wt - hi.astype(F32)).astype(BF16)
    return jnp.stack([hi, lo])


def kernel(x, c, ada_w, ada_b, norm_mix, norm_ffn, w_in, ssd_conv_w, ssd_conv_b, ssd_dt_bias, ssd_a_log, ssd_d, ssd_norm, hgrn_lb, hgrn_norm, lru_conv_w, lru_conv_b, lru_wa, lru_ba, lru_wx, lru_bx, lru_lambda, w_br_ssd, w_br_hgrn, w_br_lru, w_out, router_w, router_b, moe_w_gu, moe_b_gu, moe_w_down, moe_b_down, final_norm):
    bsz, seq, d = x.shape
    depth = ada_w.shape[0]
    a = dict(seq=seq, norm_mix=norm_mix, norm_ffn=norm_ffn, w_in=w_in, ssd_conv_w=ssd_conv_w,
             ssd_conv_b=ssd_conv_b, ssd_dt_bias=ssd_dt_bias, ssd_a_log=ssd_a_log, ssd_d=ssd_d,
             ssd_norm=ssd_norm, hgrn_norm=hgrn_norm, lru_conv_w=lru_conv_w, lru_conv_b=lru_conv_b,
             lru_wa=lru_wa, lru_ba=lru_ba, lru_wx=lru_wx, lru_bx=lru_bx, lru_lambda=lru_lambda,
             w_br_ssd=w_br_ssd, w_br_hgrn=w_br_hgrn, w_br_lru=w_br_lru, w_out=w_out, router_w=router_w,
             router_b=router_b,
             w_gu_all=moe_w_gu.astype(BF16).reshape((-1,) + moe_w_gu.shape[2:]),
             b_gu_all=moe_b_gu.reshape(-1, 1, moe_b_gu.shape[-1]),
             w_down_all=moe_w_down.astype(BF16).reshape((-1,) + moe_w_down.shape[2:]),
             b_down_all=moe_b_down.reshape(-1, 1, moe_b_down.shape[-1]))
    lb_soft = jax.nn.softmax(hgrn_lb.astype(F32), axis=0)
    lower_bounds = jnp.cumsum(lb_soft, axis=0) - lb_soft[0]
    mod = _adaln(c, ada_w, ada_b).reshape(depth, bsz, 6, 1, d)
    x2 = x.reshape(bsz * seq, d)
    fw = final_norm.reshape(1, d)
    for l in range(depth):
        p = _layer_params(l, a, lower_bounds[l])
        sh1, sc1, g1, sh2, sc2, g2 = [mod[l, :, k] for k in range(6)]
        proj = _inproj(x2, sc1, sh1, p["norm_mix"], p["w_in"], seq)
        y_ssd = _ssd(proj, bsz, seq, p)
        y_hgrn = _hgrn(proj, bsz, seq, p)
        y_lru = _lru(proj, bsz, seq, p)
        x2, v, top_idx, top_w, rank, counts = _merge(y_ssd, y_hgrn, y_lru, proj, x2, g1, sc2, sh2, p, seq)
        x2 = _moe(v, top_idx, top_w, rank, counts, x2, g2, fw, p, seq, l, final_norm=(l == depth - 1))
    return x2.reshape(bsz, seq, d)
```
